```python
import jax, jax.numpy as jnp
from jax import lax
import numpy as np

D_MODEL = 1024
BATCH = 32
SEQ = 2048
DEPTH = 2

HG_HEADS = 4
HG_DIM = 128
HG_WIDTH = HG_HEADS * HG_DIM
HG_CHUNK = 64
NSA_HEADS = 8
NSA_KV_HEADS = 2
NSA_GROUP = NSA_HEADS // NSA_KV_HEADS
NSA_DIM = 64
NSA_WIDTH = NSA_HEADS * NSA_DIM
NSA_KV_WIDTH = NSA_KV_HEADS * NSA_DIM
CMP_LEN = 32
CMP_STRIDE = 16
CMP_HIDDEN = 256
SLC_LEN = 64
SLC_TOP = 16
SLC_QBLOCK = 32
WINDOW = 512
SWA_QBLOCK = 128
N_BRANCH = 3
MIX_WIDTH = HG_WIDTH + NSA_WIDTH
D_FF = ((8 * D_MODEL + 3 * 256 - 1) // (3 * 256)) * 256
ROPE_THETA = 10000.0
RMS_EPS = 1e-6
NEG = -1e30
FORCED_SCORE = 1e9
IN_SPLITS = (HG_WIDTH, HG_WIDTH, HG_WIDTH, HG_WIDTH, NSA_WIDTH,
             NSA_KV_WIDTH, NSA_KV_WIDTH, NSA_KV_WIDTH, NSA_KV_WIDTH, NSA_KV_WIDTH, NSA_KV_WIDTH,
             NSA_HEADS * N_BRANCH)
IN_WIDTH = sum(IN_SPLITS)
IN_SPLIT_POINTS = tuple(int(v) for v in np.cumsum(IN_SPLITS)[:-1])

kernel_name = 'hymba_hgrn2_nsa_swiglu_trunk'


def rms_norm(x, gain):
    xf = x.astype(jnp.float32)
    y = xf * lax.rsqrt(jnp.mean(xf * xf, axis=-1, keepdims=True) + RMS_EPS)
    return (y * gain.astype(jnp.float32)).astype(x.dtype)


def rope(x, pos):
    half = x.shape[-1] // 2
    freqs = ROPE_THETA ** (-jnp.arange(half, dtype=jnp.float32) / half)
    ang = pos.astype(jnp.float32)[:, None] * freqs[None, :]
    cos = jnp.cos(ang)[:, None, :]
    sin = jnp.sin(ang)[:, None, :]
    xf = x.astype(jnp.float32)
    x1, x2 = xf[..., :half], xf[..., half:]
    return jnp.concatenate([x1 * cos - x2 * sin, x1 * sin + x2 * cos], axis=-1).astype(x.dtype)


def hgrn2_mixer(q, f_logit, i, g, lower_bound, g_norm):
    B, T, _ = q.shape
    dt = q.dtype
    n_chunk = T // HG_CHUNK
    lb = lower_bound.astype(jnp.float32)
    f = lb + (1.0 - lb) * jax.nn.sigmoid(f_logit.astype(jnp.float32))
    log_f = jnp.log(f)
    k = 1.0 - f
    qf = jax.nn.silu(q.astype(jnp.float32))
    v = i.astype(jnp.float32)

    def to_chunks(a):
        return a.reshape(B, n_chunk, HG_CHUNK, HG_HEADS, HG_DIM).transpose(1, 0, 3, 2, 4)

    causal = jnp.tril(jnp.ones((HG_CHUNK, HG_CHUNK), dtype=bool))[:, :, None]

    def step(S, xs):
        qc, kc, vc, lfc = xs
        b = jnp.cumsum(lfc, axis=2)
        diff = b[:, :, :, None, :] - b[:, :, None, :, :]
        decay = jnp.exp(jnp.where(causal, diff, -jnp.inf))
        A = jnp.einsum('bhtd,bhsd,bhtsd->bhts', qc, kc, decay)
        o = (jnp.einsum('bhts,bhsv->bhtv', A, vc)
             + jnp.einsum('bhtd,bhdv->bhtv', qc * jnp.exp(b), S))
        b_last = b[:, :, -1:, :]
        S = (jnp.exp(b_last[:, :, 0, :])[..., None] * S
             + jnp.einsum('bhsd,bhsv->bhdv', kc * jnp.exp(b_last - b), vc))
        return S, o

    S0 = jnp.zeros((B, HG_HEADS, HG_DIM, HG_DIM), jnp.float32)
    _, o = lax.scan(step, S0, (to_chunks(qf), to_chunks(k), to_chunks(v), to_chunks(log_f)))
    o = o.transpose(1, 0, 3, 2, 4).reshape(B, T, HG_HEADS, HG_DIM)
    o = rms_norm(o, g_norm) * jax.nn.silu(g.astype(jnp.float32).reshape(B, T, HG_HEADS, HG_DIM))
    return o.reshape(B, T, HG_WIDTH).astype(dt)


def compress(a, pe, w1, w2):
    B, T, G, d = a.shape
    R = CMP_LEN // CMP_STRIDE
    n_cmp = (T - CMP_LEN) // CMP_STRIDE + 1
    seg = a.reshape(B, T // CMP_STRIDE, CMP_STRIDE, G, d)
    blocks = jnp.concatenate([seg[:, r:r + n_cmp] for r in range(R)], axis=2)
    blocks = blocks + pe[None, None, :, None, :]
    flat = blocks.transpose(0, 1, 3, 2, 4).reshape(B, n_cmp, G, CMP_LEN * d)
    return jax.nn.silu(flat @ w1) @ w2


def cmp_to_slc_matrix(T):
    n_cmp = (T - CMP_LEN) // CMP_STRIDE + 1
    n_slc = T // SLC_LEN
    c_start = np.arange(n_cmp) * CMP_STRIDE
    c_end = c_start + CMP_LEN - 1
    s_start = np.arange(n_slc) * SLC_LEN
    s_end = s_start + SLC_LEN - 1
    M = (c_start[:, None] <= s_end[None, :]) & (c_end[:, None] >= s_start[None, :])
    return jnp.asarray(M.astype(np.float32))


def nsa_mixer(q, k_cmp, v_cmp, k_slc, v_slc, k_swa, v_swa, gate_logit, q_gain, k_gain,
              pe_k, w1_k, w2_k, pe_v, w1_v, w2_v):
    B, T, _ = q.shape
    dt = q.dtype
    H, G, Hg, d = NSA_HEADS, NSA_KV_HEADS, NSA_GROUP, NSA_DIM
    pos = jnp.arange(T)
    scale = d ** -0.5
    qr = rope(rms_norm(q.reshape(B, T, H, d), q_gain), pos)
    qg = qr.reshape(B, T, G, Hg, d)

    def kv_heads(a):
        return a.reshape(B, T, G, d)

    kc = compress(kv_heads(k_cmp), pe_k, w1_k, w2_k)
    vc = compress(kv_heads(v_cmp), pe_v, w1_v, w2_v)
    n_cmp = kc.shape[1]
    cmp_end = jnp.arange(n_cmp) * CMP_STRIDE + CMP_LEN - 1
    kc = rope(rms_norm(kc, k_gain[0]), cmp_end)
    s = jnp.einsum('btghd,bngd->bghtn', qg, kc).astype(jnp.float32) * scale
    m_cmp = cmp_end[None, :] <= pos[:, None]
    p_cmp = jax.nn.softmax(jnp.where(m_cmp, s, NEG), axis=-1) * m_cmp
    o_cmp = jnp.einsum('bghtn,bngd->btghd', p_cmp.astype(dt), vc)

    n_slc = T // SLC_LEN
    imp = jnp.einsum('bghtn,nj->bgtj', p_cmp, cmp_to_slc_matrix(T))
    blk = jnp.arange(n_slc)
    cur = pos // SLC_LEN
    forced = (blk[None, :] == 0) | (blk[None, :] == cur[:, None]) | (blk[None, :] == cur[:, None] - 1)
    causal_blk = blk[None, :] <= cur[:, None]
    score = jnp.where(forced, FORCED_SCORE, jnp.where(causal_blk, imp, NEG))
    top = min(SLC_TOP, n_slc)
    _, idx = lax.top_k(score, top)

    ks = rope(rms_norm(kv_heads(k_slc), k_gain[1]), pos)
    kb = ks.reshape(B, n_slc, SLC_LEN, G, d).transpose(0, 3, 1, 2, 4)
    vb = kv_heads(v_slc).reshape(B, n_slc, SLC_LEN, G, d).transpose(0, 3, 1, 2, 4)
    nq = T // SLC_QBLOCK
    q_blocks = qg.reshape(B, nq, SLC_QBLOCK, G, Hg, d).transpose(1, 0, 3, 4, 2, 5)
    idx_blocks = idx.reshape(B, G, nq, SLC_QBLOCK, top).transpose(2, 0, 1, 3, 4)
    t_blocks = pos.reshape(nq, SLC_QBLOCK)
    bi = jnp.arange(B)[:, None, None, None]
    gi = jnp.arange(G)[None, :, None, None]
    tok = jnp.arange(SLC_LEN)

    def slc_block(xs):
        qb, ib, tb = xs
        kg = kb[bi, gi, ib]
        vg = vb[bi, gi, ib]
        sb = jnp.einsum('bghqd,bgqnsd->bghqns', qb, kg).astype(jnp.float32) * scale
        kpos = ib[..., None] * SLC_LEN + tok
        mb = (kpos <= tb[None, None, :, None, None])[:, :, None]
        sb = jnp.where(mb, sb, NEG)
        pb = jax.nn.softmax(sb.reshape(B, G, Hg, SLC_QBLOCK, -1), axis=-1).reshape(sb.shape)
        return jnp.einsum('bghqns,bgqnsd->bghqd', pb.astype(dt), vg)

    o_slc = lax.map(slc_block, (q_blocks, idx_blocks, t_blocks))
    o_slc = o_slc.transpose(1, 0, 4, 2, 3, 5).reshape(B, T, G, Hg, d)

    kw = rope(rms_norm(kv_heads(k_swa), k_gain[2]), pos)
    vw = kv_heads(v_swa)
    kpad = jnp.pad(kw, ((0, 0), (WINDOW, 0), (0, 0), (0, 0)))
    vpad = jnp.pad(vw, ((0, 0), (WINDOW, 0), (0, 0), (0, 0)))
    nw = T // SWA_QBLOCK
    span = WINDOW + SWA_QBLOCK
    qw_blocks = qg.reshape(B, nw, SWA_QBLOCK, G, Hg, d).transpose(1, 0, 3, 4, 2, 5)

    def swa_block(xs):
        qb, n = xs
        start = n * SWA_QBLOCK
        kblk = lax.dynamic_slice_in_dim(kpad, start, span, axis=1)
        vblk = lax.dynamic_slice_in_dim(vpad, start, span, axis=1)
        tq = start + jnp.arange(SWA_QBLOCK)
        tk = start - WINDOW + jnp.arange(span)
        mw = (tk[None, :] <= tq[:, None]) & (tk[None, :] > tq[:, None] - WINDOW) & (tk[None, :] >= 0)
        sw = jnp.einsum('bghqd,bkgd->bghqk', qb, kblk).astype(jnp.float32) * scale
        pw = jax.nn.softmax(jnp.where(mw, sw, NEG), axis=-1)
        return jnp.einsum('bghqk,bkgd->bghqd', pw.astype(dt), vblk)

    o_swa = lax.map(swa_block, (qw_blocks, jnp.arange(nw)))
    o_swa = o_swa.transpose(1, 0, 4, 2, 3, 5).reshape(B, T, G, Hg, d)

    gate = jax.nn.sigmoid(gate_logit.astype(jnp.float32)).reshape(B, T, G, Hg, N_BRANCH)
    o = (gate[..., 0:1] * o_cmp.astype(jnp.float32)
         + gate[..., 1:2] * o_slc.astype(jnp.float32)
         + gate[..., 2:3] * o_swa.astype(jnp.float32))
    return o.reshape(B, T, NSA_WIDTH).astype(dt)


def setup_inputs(seed: int = 0) -> dict:
    key = jax.random.key(seed)
    ks = jax.random.split(key, 20)

    def dense(k, shape, fan_in):
        return jax.random.normal(k, shape, jnp.float32) * fan_in ** -0.5

    def gain(k, shape):
        return 1.0 + 0.05 * jax.random.normal(k, shape, jnp.float32)

    return {
        'x': jax.random.normal(ks[0], (BATCH, SEQ, D_MODEL), jnp.float32),
        'w_in': dense(ks[1], (DEPTH, D_MODEL, IN_WIDTH), D_MODEL),
        'w_out': dense(ks[2], (DEPTH, MIX_WIDTH, D_MODEL), MIX_WIDTH),
        'hg_lb_logits': 0.1 * jax.random.normal(ks[3], (DEPTH, HG_WIDTH), jnp.float32),
        'hg_gnorm': gain(ks[4], (DEPTH, HG_DIM)),
        'q_gain': gain(ks[5], (DEPTH, NSA_DIM)),
        'k_gain': gain(ks[6], (DEPTH, N_BRANCH, NSA_DIM)),
        'cmp_pe_k': 0.1 * jax.random.normal(ks[7], (DEPTH, CMP_LEN, NSA_DIM), jnp.float32),
        'cmp_w1_k': dense(ks[8], (DEPTH, CMP_LEN * NSA_DIM, CMP_HIDDEN), CMP_LEN * NSA_DIM),
        'cmp_w2_k': dense(ks[9], (DEPTH, CMP_HIDDEN, NSA_DIM), CMP_HIDDEN),
        'cmp_pe_v': 0.1 * jax.random.normal(ks[10], (DEPTH, CMP_LEN, NSA_DIM), jnp.float32),
        'cmp_w1_v': dense(ks[11], (DEPTH, CMP_LEN * NSA_DIM, CMP_HIDDEN), CMP_LEN * NSA_DIM),
        'cmp_w2_v': dense(ks[12], (DEPTH, CMP_HIDDEN, NSA_DIM), CMP_HIDDEN),
        'w_ffn_in': dense(ks[13], (DEPTH, D_MODEL, 2 * D_FF), D_MODEL),
        'w_ffn_out': dense(ks[14], (DEPTH, D_FF, D_MODEL), D_FF),
        'norm_mix': gain(ks[15], (DEPTH, D_MODEL)),
        'norm_ffn': gain(ks[16], (DEPTH, D_MODEL)),
    }


def reference(x, w_in, w_out, hg_lb_logits, hg_gnorm, q_gain, k_gain,
              cmp_pe_k, cmp_w1_k, cmp_w2_k, cmp_pe_v, cmp_w1_v, cmp_w2_v,
              w_ffn_in, w_ffn_out, norm_mix, norm_ffn):
    lb = jnp.cumsum(jax.nn.softmax(hg_lb_logits.astype(jnp.float32), axis=0), axis=0)
    lb = lb - lb[0:1]
    h = x
    for l in range(DEPTH):
        xn = rms_norm(h, norm_mix[l])
        proj = xn @ w_in[l]
        (hq, hf, hi, hg, nq, kc, vc, ksl, vsl, ksw, vsw, ngate) = jnp.split(proj, IN_SPLIT_POINTS, axis=-1)
        o_hg = hgrn2_mixer(hq, hf, hi, hg, lb[l], hg_gnorm[l])
        o_nsa = nsa_mixer(nq, kc, vc, ksl, vsl, ksw, vsw, ngate, q_gain[l], k_gain[l],
                          cmp_pe_k[l], cmp_w1_k[l], cmp_w2_k[l], cmp_pe_v[l], cmp_w1_v[l], cmp_w2_v[l])
        mix = jnp.concatenate([o_hg, o_nsa], axis=-1)
        h = h + mix @ w_out[l]
        xn = rms_norm(h, norm_ffn[l])
        gu = xn @ w_ffn_in[l]
        g_ff, u_ff = gu[..., :D_FF], gu[..., D_FF:]
        h = h + (jax.nn.silu(g_ff) * u_ff) @ w_ffn_out[l]
    return h
```

```python
import functools

import jax
import jax.numpy as jnp
import numpy as np
from jax import lax
from jax.experimental import pallas as pl
from jax.experimental.pallas import tpu as pltpu

F32 = jnp.float32
BF16 = jnp.bfloat16

D_MODEL = 1024
HG_HEADS = 4
HG_DIM = 128
HG_WIDTH = HG_HEADS * HG_DIM
NSA_HEADS = 8
NSA_KV_HEADS = 2
NSA_GROUP = NSA_HEADS // NSA_KV_HEADS
NSA_DIM = 64
NSA_WIDTH = NSA_HEADS * NSA_DIM
NSA_KV_WIDTH = NSA_KV_HEADS * NSA_DIM
CMP_LEN = 32
CMP_STRIDE = 16
CMP_HIDDEN = 256
SLC_LEN = 64
SLC_TOP = 16
WINDOW = 512
N_BRANCH = 3
D_FF = 2816
ROPE_THETA = 10000.0
RMS_EPS = 1e-6
NEG = -1e30
FORCED_SCORE = 1e9

LANES = 128
V7X_VMEM_LIMIT = 56 * 1024 * 1024

PROJ_ROWS = 512
HG_ROWS = 256
HG_CHUNK = 32
HG_SAFE_DECAY = 60.0
PREP_ROWS = 512
ATT_Q = 128
ATT_K = 256
FF_TILE = 1408

_HG_END = 4 * HG_WIDTH
_NQ_END = _HG_END + NSA_WIDTH
_KC_END = _NQ_END + NSA_KV_WIDTH
_VC_END = _KC_END + NSA_KV_WIDTH
_KV_END = _VC_END + 4 * NSA_KV_WIDTH
_GATE_END = _KV_END + NSA_KV_HEADS * LANES
GATES_PER_GROUP = NSA_GROUP * N_BRANCH


def _dot(a, b):
    return jnp.dot(a, b, preferred_element_type=F32)


def _dot_nt(a, b):
    return lax.dot_general(a, b, (((1,), (1,)), ((), ())), preferred_element_type=F32)


def _split3(x):
    hi = x.astype(BF16)
    r1 = x - hi.astype(F32)
    mid = r1.astype(BF16)
    lo = (r1 - mid.astype(F32)).astype(BF16)
    return hi, mid, lo


def _sigmoid(x):
    return 1.0 / (1.0 + jnp.exp(-x))


def _silu(x):
    return x * _sigmoid(x)


def _params(semantics):
    return pltpu.CompilerParams(dimension_semantics=semantics, vmem_limit_bytes=V7X_VMEM_LIMIT)


def _norm_proj_kernel(x_ref, gain_ref, w_ref, hg_ref, nq_ref, kc_ref, vc_ref, kv_ref, gate_ref):
    x = x_ref[...]
    ms = jnp.mean(x * x, axis=-1, keepdims=True)
    xn = (x * lax.rsqrt(ms + RMS_EPS) * gain_ref[...]).astype(BF16)
    bounds = (0, _HG_END, _NQ_END, _KC_END, _VC_END, _KV_END, _GATE_END)
    outs = (hg_ref, nq_ref, kc_ref, vc_ref, kv_ref, gate_ref)
    for o_ref, a, b in zip(outs, bounds[:-1], bounds[1:]):
        o_ref[...] = _dot(xn, w_ref[:, a:b])


def _norm_proj(h, gain, w_pad):
    n = h.shape[0]
    tm = PROJ_ROWS
    widths = (_HG_END, NSA_WIDTH, NSA_KV_WIDTH, NSA_KV_WIDTH, 4 * NSA_KV_WIDTH, NSA_KV_HEADS * LANES)
    return pl.pallas_call(
        _norm_proj_kernel,
        grid=(n // tm,),
        in_specs=[
            pl.BlockSpec((tm, D_MODEL), lambda i: (i, 0)),
            pl.BlockSpec((1, D_MODEL), lambda i: (0, 0)),
            pl.BlockSpec((D_MODEL, _GATE_END), lambda i: (0, 0)),
        ],
        out_specs=[pl.BlockSpec((tm, w), lambda i: (i, 0)) for w in widths],
        out_shape=[jax.ShapeDtypeStruct((n, w), F32) for w in widths],
        compiler_params=_params(("parallel",)),
        name="norm_proj",
    )(h, gain, w_pad)


def _hgrn_kernel(q_ref, f_ref, i_ref, g_ref, lbl_ref, gn_ref, tri_ref, o_ref,
                 s_ref, b_ref, k_ref, *, layer, depth):
    rows = q_ref.shape[0]
    n_chunks = rows // HG_CHUNK
    c = HG_CHUNK

    @pl.when(pl.program_id(1) == 0)
    def _():
        s_ref[...] = jnp.zeros_like(s_ref)

    lg = lbl_ref[...]
    rows_l = [lg[i:i + 1, :] for i in range(depth)]
    mx = functools.reduce(jnp.maximum, rows_l)
    ex = [jnp.exp(r - mx) for r in rows_l]
    den = functools.reduce(lambda a, b_: a + b_, ex)
    pr = [e / den for e in ex]
    lb = functools.reduce(lambda a, b_: a + b_, pr[:layer + 1]) - pr[0]

    f = lb + (1.0 - lb) * _sigmoid(f_ref[...])
    k_ref[...] = 1.0 - f
    hi, mid, lo = _split3(jnp.log(f))
    tri = tri_ref[...]
    b_all = _dot(tri, hi) + _dot(tri, mid) + _dot(tri, lo)
    b_ref[...] = b_all
    needs_exact = jnp.min(b_all) < -HG_SAFE_DECAY

    t_io = lax.broadcasted_iota(jnp.int32, (c, c), 0)
    s_io = lax.broadcasted_iota(jnp.int32, (c, c), 1)
    row_io = lax.broadcasted_iota(jnp.int32, (c, HG_DIM), 0)
    gn = gn_ref[...]

    def chunk_step(ci, exact):
        r0 = pl.multiple_of(ci * c, c)
        for h in range(HG_HEADS):
            cs = slice(h * HG_DIM, (h + 1) * HG_DIM)
            q = q_ref[pl.ds(r0, c), cs]
            qf = _silu(q)
            b = b_ref[pl.ds(r0, c), cs]
            k = k_ref[pl.ds(r0, c), cs]
            v = i_ref[pl.ds(r0, c), cs]
            b_last = b[c - 1:c, :]
            qt = qf * jnp.exp(b)
            khat = k * jnp.exp(b_last - b)
            if exact:
                a = jnp.zeros((c, c), F32)
                for s in range(c):
                    d = jnp.where(row_io >= s, b - b[s:s + 1, :], NEG)
                    col = jnp.sum(qf * k[s:s + 1, :] * jnp.exp(d), axis=-1, keepdims=True)
                    a = jnp.where(s_io == s, col, a)
            else:
                kt = k * jnp.exp(-b)
                a = jnp.where(s_io <= t_io, _dot_nt(qt.astype(BF16), kt.astype(BF16)), 0.0)
            st = s_ref[h]
            vb = v.astype(BF16)
            o = _dot(a.astype(BF16), vb) + _dot_nt(qt.astype(BF16), st.astype(BF16))
            s_ref[h] = st * jnp.exp(b_last) + _dot(v.T.astype(BF16), khat.astype(BF16))
            ms = jnp.mean(o * o, axis=-1, keepdims=True)
            y = o * lax.rsqrt(ms + RMS_EPS) * gn
            g = g_ref[pl.ds(r0, c), cs]
            o_ref[pl.ds(r0, c), cs] = (y * _silu(g)).astype(o_ref.dtype)

    @pl.when(jnp.logical_not(needs_exact))
    def _():
        lax.fori_loop(0, n_chunks, lambda ci, _: chunk_step(ci, False), None)

    @pl.when(needs_exact)
    def _():
        lax.fori_loop(0, n_chunks, lambda ci, _: chunk_step(ci, True), None)


def _hgrn(hgp, lb_logits, gnorm, layer, batch, seq):
    n = hgp.shape[0]
    rows = min(HG_ROWS, seq)
    nt = seq // rows
    depth = lb_logits.shape[0]
    idx = np.arange(rows)
    tri = ((idx[:, None] >= idx[None, :]) & (idx[:, None] // HG_CHUNK == idx[None, :] // HG_CHUNK))
    tri = jnp.asarray(tri.astype(np.float32), BF16)

    def col_spec(j):
        return pl.BlockSpec((rows, HG_WIDTH), lambda b, c, j=j: (b * nt + c, j))

    return pl.pallas_call(
        functools.partial(_hgrn_kernel, layer=layer, depth=depth),
        grid=(batch, nt),
        in_specs=[
            col_spec(0), col_spec(1), col_spec(2), col_spec(3),
            pl.BlockSpec((depth, HG_WIDTH), lambda b, c: (0, 0)),
            pl.BlockSpec((1, HG_DIM), lambda b, c: (0, 0)),
            pl.BlockSpec((rows, rows), lambda b, c: (0, 0)),
        ],
        out_specs=pl.BlockSpec((rows, HG_WIDTH), lambda b, c: (b * nt + c, 0)),
        out_shape=jax.ShapeDtypeStruct((n, HG_WIDTH), BF16),
        scratch_shapes=[
            pltpu.VMEM((HG_HEADS, HG_DIM, HG_DIM), F32),
            pltpu.VMEM((rows, HG_WIDTH), F32),
            pltpu.VMEM((rows, HG_WIDTH), F32),
        ],
        compiler_params=_params(("parallel", "arbitrary")),
        name="hgrn",
    )(hgp, hgp, hgp, hgp, lb_logits, gnorm, tri)


def _group_rms(x, bd, gain):
    sq = x * x
    hi = sq.astype(BF16)
    lo = (sq - hi.astype(F32)).astype(BF16)
    ss = _dot(hi, bd) + _dot(lo, bd)
    return x * lax.rsqrt(ss * (1.0 / NSA_DIM) + RMS_EPS) * gain


def _rope_lanes(x, cos, sin_signed):
    w = x.shape[-1]
    half = NSA_DIM // 2
    lane = lax.broadcasted_iota(jnp.int32, x.shape, 1)
    first = (lane & (NSA_DIM - 1)) < half
    swapped = jnp.where(first, pltpu.roll(x, w - half, 1), pltpu.roll(x, half, 1))
    return x * cos + swapped * sin_signed


def _nsa_prep_kernel(nq_ref, kv_ref, qg_ref, kg_ref, cos_ref, sin_ref, bd_ref,
                     q_out, ks_out, vs_out, kw_out, vw_out):
    cos = cos_ref[...]
    sin = sin_ref[...]
    bd = bd_ref[...]
    kvw = NSA_KV_WIDTH
    q = _rope_lanes(_group_rms(nq_ref[...], bd, qg_ref[...]), cos, sin) * (NSA_DIM ** -0.5)
    for h in range(NSA_HEADS):
        q_out[0, h] = q[:, h * NSA_DIM:(h + 1) * NSA_DIM].astype(BF16)
    kv = kv_ref[...]
    bdk = bd[:kvw, :kvw]
    cosk = cos[:, :kvw]
    sink = sin[:, :kvw]
    ks = _rope_lanes(_group_rms(kv[:, 0:kvw], bdk, kg_ref[0:1, :]), cosk, sink)
    kw = _rope_lanes(_group_rms(kv[:, 2 * kvw:3 * kvw], bdk, kg_ref[1:2, :]), cosk, sink)
    vs = kv[:, kvw:2 * kvw]
    vw = kv[:, 3 * kvw:4 * kvw]
    for g in range(NSA_KV_HEADS):
        sl = slice(g * NSA_DIM, (g + 1) * NSA_DIM)
        ks_out[0, g] = ks[:, sl].astype(BF16)
        vs_out[0, g] = vs[:, sl].astype(BF16)
        kw_out[0, g] = kw[:, sl].astype(BF16)
        vw_out[0, g] = vw[:, sl].astype(BF16)


def _rope_tables(pos, reps):
    half = NSA_DIM // 2
    freqs = ROPE_THETA ** (-jnp.arange(half, dtype=F32) / half)
    ang = pos.astype(F32)[:, None] * freqs[None, :]
    cos = jnp.cos(ang)
    sin = jnp.sin(ang)
    cos_t = jnp.tile(jnp.concatenate([cos, cos], axis=1), (1, reps))
    sin_t = jnp.tile(jnp.concatenate([-sin, sin], axis=1), (1, reps))
    return cos_t, sin_t


def _nsa_prep(nq, kv4, q_gain, k_gain, batch, seq):
    rows = min(PREP_ROWS, seq)
    nt = seq // rows
    cos_t, sin_t = _rope_tables(jnp.arange(seq), NSA_HEADS)
    gid = np.arange(NSA_WIDTH) // NSA_DIM
    bd = jnp.asarray((gid[:, None] == gid[None, :]).astype(np.float32), BF16)
    qg = jnp.tile(q_gain, NSA_HEADS)[None, :]
    kg = jnp.stack([jnp.tile(k_gain[1], NSA_KV_HEADS), jnp.tile(k_gain[2], NSA_KV_HEADS)])
    kv_shape = jax.ShapeDtypeStruct((batch, NSA_KV_HEADS, seq, NSA_DIM), BF16)
    kv_spec = pl.BlockSpec((1, NSA_KV_HEADS, rows, NSA_DIM), lambda b, c: (b, 0, c, 0))
    return pl.pallas_call(
        _nsa_prep_kernel,
        grid=(batch, nt),
        in_specs=[
            pl.BlockSpec((rows, NSA_WIDTH), lambda b, c: (b * nt + c, 0)),
            pl.BlockSpec((rows, 4 * NSA_KV_WIDTH), lambda b, c: (b * nt + c, 0)),
            pl.BlockSpec((1, NSA_WIDTH), lambda b, c: (0, 0)),
            pl.BlockSpec((2, NSA_KV_WIDTH), lambda b, c: (0, 0)),
            pl.BlockSpec((rows, NSA_WIDTH), lambda b, c: (c, 0)),
            pl.BlockSpec((rows, NSA_WIDTH), lambda b, c: (c, 0)),
            pl.BlockSpec((NSA_WIDTH, NSA_WIDTH), lambda b, c: (0, 0)),
        ],
        out_specs=[
            pl.BlockSpec((1, NSA_HEADS, rows, NSA_DIM), lambda b, c: (b, 0, c, 0)),
            kv_spec, kv_spec, kv_spec, kv_spec,
        ],
        out_shape=[
            jax.ShapeDtypeStruct((batch, NSA_HEADS, seq, NSA_DIM), BF16),
            kv_shape, kv_shape, kv_shape, kv_shape,
        ],
        compiler_params=_params(("parallel", "parallel")),
        name="nsa_prep",
    )(nq, kv4, qg, kg, cos_t, sin_t, bd)


def _nsa_cmp_kernel(kseg_ref, vseg_ref, pek_ref, pev_ref, w1k_ref, w1v_ref, w2k_ref, w2v_ref,
                    kg_ref, cos_ref, sin_ref, kc_out, vc_out):
    nseg = kseg_ref.shape[1]

    def summarize(seg, pe_ref, w1_ref, w2_ref):
        y0 = _dot((seg + pe_ref[0:1, :]).astype(BF16), w1_ref[0])
        y1 = _dot((seg + pe_ref[1:2, :]).astype(BF16), w1_ref[1])
        act = _silu(y0 + pltpu.roll(y1, nseg - 1, 0)).astype(BF16)
        w2 = w2_ref[...]
        return [_dot(act[:, g * CMP_HIDDEN:(g + 1) * CMP_HIDDEN], w2) for g in range(NSA_KV_HEADS)]

    kcs = summarize(kseg_ref[0], pek_ref, w1k_ref, w2k_ref)
    vcs = summarize(vseg_ref[0], pev_ref, w1v_ref, w2v_ref)
    half = NSA_DIM // 2
    for g in range(NSA_KV_HEADS):
        x = kcs[g]
        ms = jnp.mean(x * x, axis=-1, keepdims=True)
        xn = x * lax.rsqrt(ms + RMS_EPS) * kg_ref[...]
        swapped = jnp.concatenate([xn[:, half:], xn[:, :half]], axis=1)
        kc_out[0, g] = (xn * cos_ref[...] + swapped * sin_ref[...]).astype(BF16)
        vc_out[0, g] = vcs[g].astype(BF16)


def _cmp_weights(pe, w1):
    pe_r = pe.reshape(2, CMP_STRIDE, 1, NSA_DIM)
    pe_seg = jnp.broadcast_to(pe_r, (2, CMP_STRIDE, NSA_KV_HEADS, NSA_DIM)).reshape(2, CMP_STRIDE * NSA_KV_WIDTH)
    w1r = w1.reshape(2, CMP_STRIDE, NSA_DIM, CMP_HIDDEN)
    eye = jnp.eye(NSA_KV_HEADS, dtype=w1.dtype)
    wbig = jnp.einsum('rldj,gh->rlgdhj', w1r, eye)
    wbig = wbig.reshape(2, CMP_STRIDE * NSA_KV_WIDTH, NSA_KV_HEADS * CMP_HIDDEN)
    return pe_seg, wbig.astype(BF16)


def _nsa_cmp(kc_tok, vc_tok, pe_k, w1_k, w2_k, pe_v, w1_v, w2_v, k_gain0, batch, seq):
    nseg = seq // CMP_STRIDE
    segw = CMP_STRIDE * NSA_KV_WIDTH
    kseg = kc_tok.reshape(batch, nseg, segw)
    vseg = vc_tok.reshape(batch, nseg, segw)
    pek, w1k = _cmp_weights(pe_k, w1_k)
    pev, w1v = _cmp_weights(pe_v, w1_v)
    cmp_end = jnp.arange(nseg) * CMP_STRIDE + CMP_LEN - 1
    cos_t, sin_t = _rope_tables(cmp_end, 1)
    hid2 = NSA_KV_HEADS * CMP_HIDDEN
    seg_spec = pl.BlockSpec((1, nseg, segw), lambda b: (b, 0, 0))
    out_spec = pl.BlockSpec((1, NSA_KV_HEADS, nseg, NSA_DIM), lambda b: (b, 0, 0, 0))
    out_shape = jax.ShapeDtypeStruct((batch, NSA_KV_HEADS, nseg, NSA_DIM), BF16)
    return pl.pallas_call(
        _nsa_cmp_kernel,
        grid=(batch,),
        in_specs=[
            seg_spec, seg_spec,
            pl.BlockSpec((2, segw), lambda b: (0, 0)),
            pl.BlockSpec((2, segw), lambda b: (0, 0)),
            pl.BlockSpec((2, segw, hid2), lambda b: (0, 0, 0)),
            pl.BlockSpec((2, segw, hid2), lambda b: (0, 0, 0)),
            pl.BlockSpec((CMP_HIDDEN, NSA_DIM), lambda b: (0, 0)),
            pl.BlockSpec((CMP_HIDDEN, NSA_DIM), lambda b: (0, 0)),
            pl.BlockSpec((1, NSA_DIM), lambda b: (0, 0)),
            pl.BlockSpec((nseg, NSA_DIM), lambda b: (0, 0)),
            pl.BlockSpec((nseg, NSA_DIM), lambda b: (0, 0)),
        ],
        out_specs=[out_spec, out_spec],
        out_shape=[out_shape, out_shape],
        compiler_params=_params(("parallel",)),
        name="nsa_cmp",
    )(kseg, vseg, pek, pev, w1k, w1v, w2_k.astype(BF16), w2_v.astype(BF16),
      k_gain0[None, :], cos_t, sin_t)


def _nsa_attn_kernel(q_ref, kc_ref, vc_ref, ks_ref, vs_ref, kw_ref, vw_ref, gate_ref,
                     mcs_ref, blk_ref, o_ref, bias_ref, *, seq, top):
    tq = q_ref.shape[2]
    tk = min(ATT_K, seq)
    m_rows = NSA_GROUP * tq
    nseg = kc_ref.shape[2]
    nslc = mcs_ref.shape[0]
    t0 = pl.program_id(2) * tq
    q = q_ref[0].reshape(m_rows, NSA_DIM)

    sc = _dot_nt(q, kc_ref[0, 0])
    row = lax.broadcasted_iota(jnp.int32, (m_rows, nseg), 0)
    blk_n = lax.broadcasted_iota(jnp.int32, (m_rows, nseg), 1)
    t_row = t0 + (row & (tq - 1))
    valid = (blk_n * CMP_STRIDE + (CMP_LEN - 1)) <= t_row
    sm = jnp.where(valid, sc, NEG)
    p = jnp.where(valid, jnp.exp(sm - jnp.max(sm, axis=-1, keepdims=True)), 0.0)
    p_cmp = p / jnp.maximum(jnp.sum(p, axis=-1, keepdims=True), 1e-30)
    o_cmp = _dot(p_cmp.astype(BF16), vc_ref[0, 0])

    p_sum = functools.reduce(lambda a, b: a + b, [p_cmp[h * tq:(h + 1) * tq] for h in range(NSA_GROUP)])
    mcs = mcs_ref[...]
    imp = functools.reduce(lambda a, b: a + b, [_dot_nt(mcs, piece) for piece in _split3(p_sum)])
    j_io = lax.broadcasted_iota(jnp.int32, (nslc, tq), 0)
    cur = (t0 + lax.broadcasted_iota(jnp.int32, (nslc, tq), 1)) // SLC_LEN
    forced = (j_io == 0) | (j_io == cur) | (j_io == cur - 1)
    score = jnp.where(forced, FORCED_SCORE, jnp.where(j_io <= cur, imp, NEG))
    rank = jnp.zeros((nslc, tq), F32)
    for jp in range(nslc):
        other = score[jp:jp + 1, :]
        ahead = (other > score) | ((other == score) & (j_io > jp))
        rank = rank + jnp.where(ahead, 1.0, 0.0)
    sel = jnp.where(rank < top, 1.0, 0.0).T.astype(BF16)
    sel_keys = _dot(sel, blk_ref[...])
    key_pos = lax.broadcasted_iota(jnp.int32, (tq, seq), 1)
    q_pos = t0 + lax.broadcasted_iota(jnp.int32, (tq, seq), 0)
    bias_ref[...] = jnp.where((sel_keys > 0.5) & (key_pos <= q_pos), 0.0, NEG)

    init = (jnp.full((m_rows, 1), NEG, F32), jnp.zeros((m_rows, 1), F32), jnp.zeros((m_rows, NSA_DIM), F32))

    def online_step(carry, s, mask, v):
        m_old, l_old, acc = carry
        m_new = jnp.maximum(m_old, jnp.max(s, axis=-1, keepdims=True))
        alpha = jnp.exp(m_old - m_new)
        pe = jnp.exp(s - m_new)
        if mask is not None:
            pe = jnp.where(mask, pe, 0.0)
        l_new = alpha * l_old + jnp.sum(pe, axis=-1, keepdims=True)
        acc = alpha * acc + _dot(pe.astype(BF16), v)
        return m_new, l_new, acc

    def slc_body(kt, carry):
        k0 = pl.multiple_of(kt * tk, tk)
        s = _dot_nt(q, ks_ref[0, 0, pl.ds(k0, tk), :])
        bias = bias_ref[:, pl.ds(k0, tk)]
        s = (s.reshape(NSA_GROUP, tq, tk) + bias[None]).reshape(m_rows, tk)
        return online_step(carry, s, None, vs_ref[0, 0, pl.ds(k0, tk), :])

    _, l_s, acc_s = lax.fori_loop(0, (t0 + tq + tk - 1) // tk, slc_body, init)
    o_slc = acc_s / l_s

    kq = lax.broadcasted_iota(jnp.int32, (tq, tk), 1)
    tqq = t0 + lax.broadcasted_iota(jnp.int32, (tq, tk), 0)

    def swa_body(kt, carry):
        k0 = pl.multiple_of(kt * tk, tk)
        s = _dot_nt(q, kw_ref[0, 0, pl.ds(k0, tk), :])
        kpos = k0 + kq
        ok = (kpos <= tqq) & (kpos > tqq - WINDOW)
        ok4 = jnp.broadcast_to(ok[None], (NSA_GROUP, tq, tk)).reshape(m_rows, tk)
        s = jnp.where(ok4, s, NEG)
        return online_step(carry, s, ok4, vw_ref[0, 0, pl.ds(k0, tk), :])

    lo_tile = jnp.maximum(t0 - (WINDOW - 1), 0) // tk
    _, l_w, acc_w = lax.fori_loop(lo_tile, (t0 + tq + tk - 1) // tk, swa_body, init)
    o_swa = acc_w / l_w

    gate = _sigmoid(gate_ref[...])
    outs = []
    for h in range(NSA_GROUP):
        rs = slice(h * tq, (h + 1) * tq)
        g0 = gate[:, N_BRANCH * h:N_BRANCH * h + 1]
        g1 = gate[:, N_BRANCH * h + 1:N_BRANCH * h + 2]
        g2 = gate[:, N_BRANCH * h + 2:N_BRANCH * h + 3]
        outs.append(g0 * o_cmp[rs] + g1 * o_slc[rs] + g2 * o_swa[rs])
    o_ref[...] = jnp.concatenate(outs, axis=1).astype(o_ref.dtype)


def _nsa_attn(q, kc, vc, ks, vs, kw, vw, gate, batch, seq):
    tq = min(ATT_Q, seq)
    nq = seq // tq
    nseg = seq // CMP_STRIDE
    nslc = seq // SLC_LEN
    top = min(SLC_TOP, nslc)
    c_start = np.arange(nseg) * CMP_STRIDE
    c_end = c_start + CMP_LEN - 1
    s_start = np.arange(nslc) * SLC_LEN
    s_end = s_start + SLC_LEN - 1
    overlap = (c_start[None, :] <= s_end[:, None]) & (c_end[None, :] >= s_start[:, None])
    overlap[:, nseg - 1] = False
    mcs = jnp.asarray(overlap.astype(np.float32), BF16)
    blk = jnp.asarray((np.arange(seq)[None, :] // SLC_LEN == np.arange(nslc)[:, None]).astype(np.float32), BF16)
    grp_w = NSA_GROUP * NSA_DIM
    kv_full = pl.BlockSpec((1, 1, seq, NSA_DIM), lambda b, g, i: (b, g, 0, 0))
    cmp_full = pl.BlockSpec((1, 1, nseg, NSA_DIM), lambda b, g, i: (b, g, 0, 0))
    return pl.pallas_call(
        functools.partial(_nsa_attn_kernel, seq=seq, top=top),
        grid=(batch, NSA_KV_HEADS, nq),
        in_specs=[
            pl.BlockSpec((1, NSA_GROUP, tq, NSA_DIM), lambda b, g, i: (b, g, i, 0)),
            cmp_full, cmp_full, kv_full, kv_full, kv_full, kv_full,
            pl.BlockSpec((tq, LANES), lambda b, g, i: (b * nq + i, g)),
            pl.BlockSpec((nslc, nseg), lambda b, g, i: (0, 0)),
            pl.BlockSpec((nslc, seq), lambda b, g, i: (0, 0)),
        ],
        out_specs=pl.BlockSpec((tq, grp_w), lambda b, g, i: (b * nq + i, g)),
        out_shape=jax.ShapeDtypeStruct((batch * seq, NSA_WIDTH), BF16),
        scratch_shapes=[pltpu.VMEM((tq, seq), F32)],
        compiler_params=_params(("parallel", "parallel", "arbitrary")),
        name="nsa_attn",
    )(q, kc, vc, ks, vs, kw, vw, gate, mcs, blk)


def _out_ffn_kernel(h_ref, ohg_ref, onsa_ref, wout_ref, gain_ref, wg_ref, wu_ref, wo_ref,
                    o_ref, xn_ref, acc_ref):
    j = pl.program_id(1)

    @pl.when(j == 0)
    def _():
        h1 = (h_ref[...] + _dot(ohg_ref[...], wout_ref[:HG_WIDTH, :])
              + _dot(onsa_ref[...], wout_ref[HG_WIDTH:, :]))
        acc_ref[...] = h1
        ms = jnp.mean(h1 * h1, axis=-1, keepdims=True)
        xn_ref[...] = (h1 * lax.rsqrt(ms + RMS_EPS) * gain_ref[...]).astype(BF16)

    xn = xn_ref[...]
    act = (_silu(_dot(xn, wg_ref[...])) * _dot(xn, wu_ref[...])).astype(BF16)
    acc_ref[...] += _dot(act, wo_ref[...])

    @pl.when(j == pl.num_programs(1) - 1)
    def _():
        o_ref[...] = acc_ref[...]


def _out_ffn(h, o_hg, o_nsa, w_out, gain, w_ffn_in, w_ffn_out):
    n = h.shape[0]
    tm = PROJ_ROWS
    nff = D_FF // FF_TILE
    mixw = HG_WIDTH + NSA_WIDTH
    return pl.pallas_call(
        _out_ffn_kernel,
        grid=(n // tm, nff),
        in_specs=[
            pl.BlockSpec((tm, D_MODEL), lambda i, j: (i, 0)),
            pl.BlockSpec((tm, HG_WIDTH), lambda i, j: (i, 0)),
            pl.BlockSpec((tm, NSA_WIDTH), lambda i, j: (i, 0)),
            pl.BlockSpec((mixw, D_MODEL), lambda i, j: (0, 0)),
            pl.BlockSpec((1, D_MODEL), lambda i, j: (0, 0)),
            pl.BlockSpec((D_MODEL, FF_TILE), lambda i, j: (0, j)),
            pl.BlockSpec((D_MODEL, FF_TILE), lambda i, j: (0, nff + j)),
            pl.BlockSpec((FF_TILE, D_MODEL), lambda i, j: (j, 0)),
        ],
        out_specs=pl.BlockSpec((tm, D_MODEL), lambda i, j: (i, 0)),
        out_shape=jax.ShapeDtypeStruct((n, D_MODEL), F32),
        scratch_shapes=[pltpu.VMEM((tm, D_MODEL), BF16), pltpu.VMEM((tm, D_MODEL), F32)],
        compiler_params=_params(("parallel", "arbitrary")),
        name="out_ffn",
    )(h, o_hg, o_nsa, w_out, gain, w_ffn_in, w_ffn_in, w_ffn_out)


def _pad_w_in(w):
    pieces = [w[:, :_KV_END]]
    for g in range(NSA_KV_HEADS):
        cols = w[:, _KV_END + g * GATES_PER_GROUP:_KV_END + (g + 1) * GATES_PER_GROUP]
        pieces.append(jnp.pad(cols, ((0, 0), (0, LANES - GATES_PER_GROUP))))
    return jnp.concatenate(pieces, axis=1).astype(BF16)


def kernel(x, w_in, w_out, hg_lb_logits, hg_gnorm, q_gain, k_gain, cmp_pe_k, cmp_w1_k, cmp_w2_k,
           cmp_pe_v, cmp_w1_v, cmp_w2_v, w_ffn_in, w_ffn_out, norm_mix, norm_ffn):
    batch, seq, _ = x.shape
    depth = w_in.shape[0]
    h = x.reshape(batch * seq, D_MODEL)
    for l in range(depth):
        hgp, nq, kc_tok, vc_tok, kv4, gate = _norm_proj(h, norm_mix[l][None, :], _pad_w_in(w_in[l]))
        o_hg = _hgrn(hgp, hg_lb_logits, hg_gnorm[l][None, :], l, batch, seq)
        q, ks, vs, kw, vw = _nsa_prep(nq, kv4, q_gain[l], k_gain[l], batch, seq)
        kc, vc = _nsa_cmp(kc_tok, vc_tok, cmp_pe_k[l], cmp_w1_k[l], cmp_w2_k[l],
                          cmp_pe_v[l], cmp_w1_v[l], cmp_w2_v[l], k_gain[l, 0], batch, seq)
        o_nsa = _nsa_attn(q, kc, vc, ks, vs, kw, vw, gate, batch, seq)
        h = _out_ffn(h, o_hg, o_nsa, w_out[l].astype(BF16), norm_ffn[l][None, :],
                     w_ffn_in[l].astype(BF16), w_ffn_out[l].astype(BF16))
    return h.reshape(batch, seq, D_MODEL)
```

```python
import functools

import jax
import jax.numpy as jnp
import numpy as np
from jax import lax
from jax.experimental import pallas as pl
from jax.experimental.pallas import tpu as pltpu

F32 = jnp.float32
BF16 = jnp.bfloat16

D_MODEL = 1024
HG_HEADS = 4
HG_DIM = 128
HG_WIDTH = HG_HEADS * HG_DIM
NSA_HEADS = 8
NSA_KV_HEADS = 2
NSA_GROUP = NSA_HEADS // NSA_KV_HEADS
NSA_DIM = 64
NSA_WIDTH = NSA_HEADS * NSA_DIM
NSA_KV_WIDTH = NSA_KV_HEADS * NSA_DIM
CMP_LEN = 32
CMP_STRIDE = 16
CMP_HIDDEN = 256
SLC_LEN = 64
SLC_TOP = 16
WINDOW = 512
N_BRANCH = 3
D_FF = 2816
ROPE_THETA = 10000.0
RMS_EPS = 1e-6
NEG = -1e30
FORCED_SCORE = 1e9
LOG2_E = 1.4426950408889634

LANES = 128
V7X_VMEM_LIMIT = 56 * 1024 * 1024

PROJ_ROWS = 512
HG_ROWS = 256
HG_CHUNK = 32
HG_SAFE_DECAY = 60.0
PREP_ROWS = 512
ATT_Q = 256
ATT_K = 256
FF_TILE = 1408
K_AUG = 2 * NSA_DIM
V_AUG = NSA_DIM + 16

_HG_END = 4 * HG_WIDTH
_NQ_END = _HG_END + NSA_WIDTH
_KC_END = _NQ_END + NSA_KV_WIDTH
_VC_END = _KC_END + NSA_KV_WIDTH
_KV_END = _VC_END + 4 * NSA_KV_WIDTH
_GATE_END = _KV_END + NSA_KV_HEADS * LANES
GATES_PER_GROUP = NSA_GROUP * N_BRANCH


def _dot(a, b):
    return jnp.dot(a, b, preferred_element_type=F32)


def _dot_nt(a, b):
    return lax.dot_general(a, b, (((1,), (1,)), ((), ())), preferred_element_type=F32)


def _split3(x):
    hi = x.astype(BF16)
    r1 = x - hi.astype(F32)
    mid = r1.astype(BF16)
    lo = (r1 - mid.astype(F32)).astype(BF16)
    return hi, mid, lo


def _sigmoid(x):
    return 1.0 / (1.0 + jnp.exp(-x))


def _silu(x):
    return x * _sigmoid(x)


def _params(semantics):
    return pltpu.CompilerParams(dimension_semantics=semantics, vmem_limit_bytes=V7X_VMEM_LIMIT)


def _norm_proj_kernel(x_ref, gain_ref, w_ref, hg_ref, nq_ref, kc_ref, vc_ref, kv_ref, gate_ref):
    x = x_ref[...]
    ms = jnp.mean(x * x, axis=-1, keepdims=True)
    xn = (x * lax.rsqrt(ms + RMS_EPS) * gain_ref[...]).astype(BF16)
    bounds = (0, _HG_END, _NQ_END, _KC_END, _VC_END, _KV_END, _GATE_END)
    outs = (hg_ref, nq_ref, kc_ref, vc_ref, kv_ref, gate_ref)
    for o_ref, a, b in zip(outs, bounds[:-1], bounds[1:]):
        o_ref[...] = _dot(xn, w_ref[:, a:b])


def _norm_proj(h, gain, w_pad):
    n = h.shape[0]
    tm = PROJ_ROWS
    widths = (_HG_END, NSA_WIDTH, NSA_KV_WIDTH, NSA_KV_WIDTH, 4 * NSA_KV_WIDTH, NSA_KV_HEADS * LANES)
    return pl.pallas_call(
        _norm_proj_kernel,
        grid=(n // tm,),
        in_specs=[
            pl.BlockSpec((tm, D_MODEL), lambda i: (i, 0)),
            pl.BlockSpec((1, D_MODEL), lambda i: (0, 0)),
            pl.BlockSpec((D_MODEL, _GATE_END), lambda i: (0, 0)),
        ],
        out_specs=[pl.BlockSpec((tm, w), lambda i: (i, 0)) for w in widths],
        out_shape=[jax.ShapeDtypeStruct((n, w), F32) for w in widths],
        compiler_params=_params(("parallel",)),
        name="norm_proj",
    )(h, gain, w_pad)


def _hgrn_kernel(q_ref, f_ref, i_ref, g_ref, lbl_ref, gn_ref, tri_ref, o_ref,
                 s_ref, b_ref, k_ref, *, layer, depth):
    rows = q_ref.shape[0]
    n_chunks = rows // HG_CHUNK
    c = HG_CHUNK

    @pl.when(pl.program_id(1) == 0)
    def _():
        s_ref[...] = jnp.zeros_like(s_ref)

    lg = lbl_ref[...]
    rows_l = [lg[i:i + 1, :] for i in range(depth)]
    mx = functools.reduce(jnp.maximum, rows_l)
    ex = [jnp.exp(r - mx) for r in rows_l]
    den = functools.reduce(lambda a, b_: a + b_, ex)
    pr = [e / den for e in ex]
    lb = functools.reduce(lambda a, b_: a + b_, pr[:layer + 1]) - pr[0]

    f = lb + (1.0 - lb) * _sigmoid(f_ref[...])
    k_ref[...] = 1.0 - f
    hi, mid, lo = _split3(jnp.log(f))
    tri = tri_ref[...]
    b_all = _dot(tri, hi) + _dot(tri, mid) + _dot(tri, lo)
    b_ref[...] = b_all
    needs_exact = jnp.min(b_all) < -HG_SAFE_DECAY

    t_io = lax.broadcasted_iota(jnp.int32, (c, c), 0)
    s_io = lax.broadcasted_iota(jnp.int32, (c, c), 1)
    row_io = lax.broadcasted_iota(jnp.int32, (c, HG_DIM), 0)
    gn = gn_ref[...]

    def chunk_step(ci, exact):
        r0 = pl.multiple_of(ci * c, c)
        for h in range(HG_HEADS):
            cs = slice(h * HG_DIM, (h + 1) * HG_DIM)
            q = q_ref[pl.ds(r0, c), cs]
            qf = _silu(q)
            b = b_ref[pl.ds(r0, c), cs]
            k = k_ref[pl.ds(r0, c), cs]
            v = i_ref[pl.ds(r0, c), cs]
            b_last = b[c - 1:c, :]
            qt = qf * jnp.exp(b)
            khat = k * jnp.exp(b_last - b)
            if exact:
                a = jnp.zeros((c, c), F32)
                for s in range(c):
                    d = jnp.where(row_io >= s, b - b[s:s + 1, :], NEG)
                    col = jnp.sum(qf * k[s:s + 1, :] * jnp.exp(d), axis=-1, keepdims=True)
                    a = jnp.where(s_io == s, col, a)
            else:
                kt = k * jnp.exp(-b)
                a = jnp.where(s_io <= t_io, _dot_nt(qt.astype(BF16), kt.astype(BF16)), 0.0)
            st = s_ref[h]
            vb = v.astype(BF16)
            o = _dot(a.astype(BF16), vb) + _dot_nt(qt.astype(BF16), st.astype(BF16))
            s_ref[h] = st * jnp.exp(b_last) + _dot(v.T.astype(BF16), khat.astype(BF16))
            ms = jnp.mean(o * o, axis=-1, keepdims=True)
            y = o * lax.rsqrt(ms + RMS_EPS) * gn
            g = g_ref[pl.ds(r0, c), cs]
            o_ref[pl.ds(r0, c), cs] = (y * _silu(g)).astype(o_ref.dtype)

    @pl.when(jnp.logical_not(needs_exact))
    def _():
        lax.fori_loop(0, n_chunks, lambda ci, _: chunk_step(ci, False), None)

    @pl.when(needs_exact)
    def _():
        lax.fori_loop(0, n_chunks, lambda ci, _: chunk_step(ci, True), None)


def _hgrn(hgp, lb_logits, gnorm, layer, batch, seq):
    n = hgp.shape[0]
    rows = min(HG_ROWS, seq)
    nt = seq // rows
    depth = lb_logits.shape[0]
    idx = np.arange(rows)
    tri = ((idx[:, None] >= idx[None, :]) & (idx[:, None] // HG_CHUNK == idx[None, :] // HG_CHUNK))
    tri = jnp.asarray(tri.astype(np.float32), BF16)

    def col_spec(j):
        return pl.BlockSpec((rows, HG_WIDTH), lambda b, c, j=j: (b * nt + c, j))

    return pl.pallas_call(
        functools.partial(_hgrn_kernel, layer=layer, depth=depth),
        grid=(batch, nt),
        in_specs=[
            col_spec(0), col_spec(1), col_spec(2), col_spec(3),
            pl.BlockSpec((depth, HG_WIDTH), lambda b, c: (0, 0)),
            pl.BlockSpec((1, HG_DIM), lambda b, c: (0, 0)),
            pl.BlockSpec((rows, rows), lambda b, c: (0, 0)),
        ],
        out_specs=pl.BlockSpec((rows, HG_WIDTH), lambda b, c: (b * nt + c, 0)),
        out_shape=jax.ShapeDtypeStruct((n, HG_WIDTH), BF16),
        scratch_shapes=[
            pltpu.VMEM((HG_HEADS, HG_DIM, HG_DIM), F32),
            pltpu.VMEM((rows, HG_WIDTH), F32),
            pltpu.VMEM((rows, HG_WIDTH), F32),
        ],
        compiler_params=_params(("parallel", "arbitrary")),
        name="hgrn",
    )(hgp, hgp, hgp, hgp, lb_logits, gnorm, tri)


def _group_rms(x, bd, gain):
    sq = x * x
    hi = sq.astype(BF16)
    lo = (sq - hi.astype(F32)).astype(BF16)
    ss = _dot(hi, bd) + _dot(lo, bd)
    return x * lax.rsqrt(ss * (1.0 / NSA_DIM) + RMS_EPS) * gain


def _rope_lanes(x, cos, sin_signed):
    w = x.shape[-1]
    half = NSA_DIM // 2
    lane = lax.broadcasted_iota(jnp.int32, x.shape, 1)
    first = (lane & (NSA_DIM - 1)) < half
    swapped = jnp.where(first, pltpu.roll(x, w - half, 1), pltpu.roll(x, half, 1))
    return x * cos + swapped * sin_signed


def _nsa_prep_kernel(nq_ref, kv_ref, qg_ref, kg_ref, cos_ref, sin_ref, bd_ref,
                     qt_out, ks_out, vst_out, kw_out, vwt_out):
    cos = cos_ref[...]
    sin = sin_ref[...]
    bd = bd_ref[...]
    kvw = NSA_KV_WIDTH
    rows = nq_ref.shape[0]
    q = _rope_lanes(_group_rms(nq_ref[...], bd, qg_ref[...]), cos, sin) * (NSA_DIM ** -0.5 * LOG2_E)
    qt_out[0] = q.T.reshape(NSA_HEADS, NSA_DIM, rows).astype(BF16)
    kv = kv_ref[...]
    bdk = bd[:kvw, :kvw]
    cosk = cos[:, :kvw]
    sink = sin[:, :kvw]
    ks = _rope_lanes(_group_rms(kv[:, 0:kvw], bdk, kg_ref[0:1, :]), cosk, sink)
    kw = _rope_lanes(_group_rms(kv[:, 2 * kvw:3 * kvw], bdk, kg_ref[1:2, :]), cosk, sink)
    pad_rows = V_AUG - NSA_DIM
    ones_row = jnp.where(lax.broadcasted_iota(jnp.int32, (NSA_KV_HEADS, pad_rows, rows), 1) == 0, 1.0, 0.0)

    def values_t(v):
        vt = v.T.reshape(NSA_KV_HEADS, NSA_DIM, rows)
        return jnp.concatenate([vt, ones_row], axis=1).astype(BF16)

    vst_out[0] = values_t(kv[:, kvw:2 * kvw])
    vwt_out[0] = values_t(kv[:, 3 * kvw:4 * kvw])
    tok = pl.program_id(1) * rows + lax.broadcasted_iota(jnp.int32, (rows, NSA_DIM), 0)
    onehot = jnp.where(lax.broadcasted_iota(jnp.int32, (rows, NSA_DIM), 1) == tok // SLC_LEN, 1.0, 0.0)
    zeros = jnp.zeros((rows, NSA_DIM), F32)
    for g in range(NSA_KV_HEADS):
        sl = slice(g * NSA_DIM, (g + 1) * NSA_DIM)
        ks_out[0, g] = jnp.concatenate([ks[:, sl], onehot], axis=1).astype(BF16)
        kw_out[0, g] = jnp.concatenate([kw[:, sl], zeros], axis=1).astype(BF16)


def _rope_tables(pos, reps):
    half = NSA_DIM // 2
    freqs = ROPE_THETA ** (-jnp.arange(half, dtype=F32) / half)
    ang = pos.astype(F32)[:, None] * freqs[None, :]
    cos = jnp.cos(ang)
    sin = jnp.sin(ang)
    cos_t = jnp.tile(jnp.concatenate([cos, cos], axis=1), (1, reps))
    sin_t = jnp.tile(jnp.concatenate([-sin, sin], axis=1), (1, reps))
    return cos_t, sin_t


def _nsa_prep(nq, kv4, q_gain, k_gain, batch, seq):
    rows = min(PREP_ROWS, seq)
    nt = seq // rows
    cos_t, sin_t = _rope_tables(jnp.arange(seq), NSA_HEADS)
    gid = np.arange(NSA_WIDTH) // NSA_DIM
    bd = jnp.asarray((gid[:, None] == gid[None, :]).astype(np.float32), BF16)
    qg = jnp.tile(q_gain, NSA_HEADS)[None, :]
    kg = jnp.stack([jnp.tile(k_gain[1], NSA_KV_HEADS), jnp.tile(k_gain[2], NSA_KV_HEADS)])
    assert seq // SLC_LEN <= NSA_DIM, "selection one-hot must fit the spare key lanes"
    k_shape = jax.ShapeDtypeStruct((batch, NSA_KV_HEADS, seq, K_AUG), BF16)
    k_spec = pl.BlockSpec((1, NSA_KV_HEADS, rows, K_AUG), lambda b, c: (b, 0, c, 0))
    vt_shape = jax.ShapeDtypeStruct((batch, NSA_KV_HEADS, V_AUG, seq), BF16)
    vt_spec = pl.BlockSpec((1, NSA_KV_HEADS, V_AUG, rows), lambda b, c: (b, 0, 0, c))
    return pl.pallas_call(
        _nsa_prep_kernel,
        grid=(batch, nt),
        in_specs=[
            pl.BlockSpec((rows, NSA_WIDTH), lambda b, c: (b * nt + c, 0)),
            pl.BlockSpec((rows, 4 * NSA_KV_WIDTH), lambda b, c: (b * nt + c, 0)),
            pl.BlockSpec((1, NSA_WIDTH), lambda b, c: (0, 0)),
            pl.BlockSpec((2, NSA_KV_WIDTH), lambda b, c: (0, 0)),
            pl.BlockSpec((rows, NSA_WIDTH), lambda b, c: (c, 0)),
            pl.BlockSpec((rows, NSA_WIDTH), lambda b, c: (c, 0)),
            pl.BlockSpec((NSA_WIDTH, NSA_WIDTH), lambda b, c: (0, 0)),
        ],
        out_specs=[
            pl.BlockSpec((1, NSA_HEADS, NSA_DIM, rows), lambda b, c: (b, 0, 0, c)),
            k_spec, vt_spec, k_spec, vt_spec,
        ],
        out_shape=[
            jax.ShapeDtypeStruct((batch, NSA_HEADS, NSA_DIM, seq), BF16),
            k_shape, vt_shape, k_shape, vt_shape,
        ],
        compiler_params=_params(("parallel", "parallel")),
        name="nsa_prep",
    )(nq, kv4, qg, kg, cos_t, sin_t, bd)


def _nsa_cmp_kernel(kseg_ref, vseg_ref, pek_ref, pev_ref, w1k_ref, w1v_ref, w2k_ref, w2v_ref,
                    kg_ref, cos_ref, sin_ref, kc_out, vct_out):
    nseg = kseg_ref.shape[1]

    def hidden(seg, pe_ref, w1_ref):
        y0 = _dot((seg + pe_ref[0:1, :]).astype(BF16), w1_ref[0])
        y1 = _dot((seg + pe_ref[1:2, :]).astype(BF16), w1_ref[1])
        act = _silu(y0 + pltpu.roll(y1, nseg - 1, 0)).astype(BF16)
        return [act[:, g * CMP_HIDDEN:(g + 1) * CMP_HIDDEN] for g in range(NSA_KV_HEADS)]

    hk = hidden(kseg_ref[0], pek_ref, w1k_ref)
    hv = hidden(vseg_ref[0], pev_ref, w1v_ref)
    half = NSA_DIM // 2
    for g in range(NSA_KV_HEADS):
        x = _dot(hk[g], w2k_ref[...])
        ms = jnp.mean(x * x, axis=-1, keepdims=True)
        xn = x * lax.rsqrt(ms + RMS_EPS) * kg_ref[...]
        swapped = jnp.concatenate([xn[:, half:], xn[:, :half]], axis=1)
        kc_out[0, g] = (xn * cos_ref[...] + swapped * sin_ref[...]).astype(BF16)
        vct_out[0, g] = _dot_nt(w2v_ref[...], hv[g]).astype(BF16)


def _cmp_weights(pe, w1):
    pe_r = pe.reshape(2, CMP_STRIDE, 1, NSA_DIM)
    pe_seg = jnp.broadcast_to(pe_r, (2, CMP_STRIDE, NSA_KV_HEADS, NSA_DIM)).reshape(2, CMP_STRIDE * NSA_KV_WIDTH)
    w1r = w1.reshape(2, CMP_STRIDE, NSA_DIM, CMP_HIDDEN)
    eye = jnp.eye(NSA_KV_HEADS, dtype=w1.dtype)
    wbig = jnp.einsum('rldj,gh->rlgdhj', w1r, eye)
    wbig = wbig.reshape(2, CMP_STRIDE * NSA_KV_WIDTH, NSA_KV_HEADS * CMP_HIDDEN)
    return pe_seg, wbig.astype(BF16)


def _nsa_cmp(kc_tok, vc_tok, pe_k, w1_k, w2_k, pe_v, w1_v, w2_v, k_gain0, batch, seq):
    nseg = seq // CMP_STRIDE
    segw = CMP_STRIDE * NSA_KV_WIDTH
    kseg = kc_tok.reshape(batch, nseg, segw)
    vseg = vc_tok.reshape(batch, nseg, segw)
    pek, w1k = _cmp_weights(pe_k, w1_k)
    pev, w1v = _cmp_weights(pe_v, w1_v)
    cmp_end = jnp.arange(nseg) * CMP_STRIDE + CMP_LEN - 1
    cos_t, sin_t = _rope_tables(cmp_end, 1)
    hid2 = NSA_KV_HEADS * CMP_HIDDEN
    seg_spec = pl.BlockSpec((1, nseg, segw), lambda b: (b, 0, 0))
    out_spec = pl.BlockSpec((1, NSA_KV_HEADS, nseg, NSA_DIM), lambda b: (b, 0, 0, 0))
    out_shape = jax.ShapeDtypeStruct((batch, NSA_KV_HEADS, nseg, NSA_DIM), BF16)
    out_t_spec = pl.BlockSpec((1, NSA_KV_HEADS, NSA_DIM, nseg), lambda b: (b, 0, 0, 0))
    out_t_shape = jax.ShapeDtypeStruct((batch, NSA_KV_HEADS, NSA_DIM, nseg), BF16)
    return pl.pallas_call(
        _nsa_cmp_kernel,
        grid=(batch,),
        in_specs=[
            seg_spec, seg_spec,
            pl.BlockSpec((2, segw), lambda b: (0, 0)),
            pl.BlockSpec((2, segw), lambda b: (0, 0)),
            pl.BlockSpec((2, segw, hid2), lambda b: (0, 0, 0)),
            pl.BlockSpec((2, segw, hid2), lambda b: (0, 0, 0)),
            pl.BlockSpec((CMP_HIDDEN, NSA_DIM), lambda b: (0, 0)),
            pl.BlockSpec((NSA_DIM, CMP_HIDDEN), lambda b: (0, 0)),
            pl.BlockSpec((1, NSA_DIM), lambda b: (0, 0)),
            pl.BlockSpec((nseg, NSA_DIM), lambda b: (0, 0)),
            pl.BlockSpec((nseg, NSA_DIM), lambda b: (0, 0)),
        ],
        out_specs=[out_spec, out_t_spec],
        out_shape=[out_shape, out_t_shape],
        compiler_params=_params(("parallel",)),
        name="nsa_cmp",
    )(kseg, vseg, pek, pev, w1k, w1v, w2_k.astype(BF16), w2_v.T.astype(BF16),
      k_gain0[None, :], cos_t, sin_t)


def _nsa_attn_kernel(qt_ref, kc_ref, vct_ref, ks_ref, vst_ref, kw_ref, vwt_ref, gate_ref,
                     mcs_ref, o_ref, *, seq, top):
    tq = qt_ref.shape[3]
    tk = min(ATT_K, seq)
    nseg = kc_ref.shape[2]
    nslc = mcs_ref.shape[0]
    t0 = pl.program_id(1) * tq
    groups = range(NSA_KV_HEADS)
    qts = [jnp.concatenate([qt_ref[0, g * NSA_GROUP + h] for h in range(NSA_GROUP)], axis=1) for g in groups]

    def tile_heads(x):
        return jnp.concatenate([x] * NSA_GROUP, axis=1)

    blk_n = lax.broadcasted_iota(jnp.int32, (nseg, tq), 0)
    t_col = t0 + lax.broadcasted_iota(jnp.int32, (nseg, tq), 1)
    valid = (blk_n * CMP_STRIDE + (CMP_LEN - 1)) <= t_col
    j_io = lax.broadcasted_iota(jnp.int32, (nslc, tq), 0)
    cur = (t0 + lax.broadcasted_iota(jnp.int32, (nslc, tq), 1)) // SLC_LEN
    forced = (j_io == 0) | (j_io == cur) | (j_io == cur - 1)
    mcs = mcs_ref[...]
    o_cmps = []
    q_aug = []
    for g in groups:
        sc = _dot(kc_ref[0, g], qts[g])
        p_heads = []
        for h in range(NSA_GROUP):
            sm = jnp.where(valid, sc[:, h * tq:(h + 1) * tq], NEG)
            p = jnp.where(valid, jnp.exp2(sm - jnp.max(sm, axis=0, keepdims=True)), 0.0)
            p_heads.append(p / jnp.maximum(jnp.sum(p, axis=0, keepdims=True), 1e-30))
        o_cmps.append(_dot(vct_ref[0, g], jnp.concatenate(p_heads, axis=1).astype(BF16)))

        p_sum = functools.reduce(lambda a, b: a + b, p_heads)
        imp = functools.reduce(lambda a, b: a + b, [_dot(mcs, piece) for piece in _split3(p_sum)])
        score = jnp.where(forced, FORCED_SCORE, jnp.where(j_io <= cur, imp, NEG))
        rank = jnp.zeros((nslc, tq), F32)
        for jp in range(nslc):
            other = score[jp:jp + 1, :]
            ahead = (other > score) | ((other == score) & (j_io > jp))
            rank = rank + jnp.where(ahead, 1.0, 0.0)
        sel_bias = jnp.where(rank < top, 0.0, NEG)
        sel_rows = jnp.concatenate([sel_bias, jnp.zeros((K_AUG - NSA_DIM - nslc, tq), F32)], axis=0)
        q_aug.append(jnp.concatenate([qts[g], tile_heads(sel_rows).astype(BF16)], axis=0))

    lanes = NSA_GROUP * tq
    init1 = (jnp.full((1, lanes), NEG, F32), jnp.zeros((V_AUG, lanes), F32))
    init = tuple((init1, init1) for _ in groups)
    k_io = lax.broadcasted_iota(jnp.int32, (tk, tq), 0)
    t_io = t0 + lax.broadcasted_iota(jnp.int32, (tk, tq), 1)
    diag = t0 // tk
    n_win = diag - jnp.maximum(t0 - (WINDOW - 1), 0) // tk + 1

    def online_step(carry, s, vt):
        m_old, acc = carry
        m_new = jnp.maximum(m_old, jnp.max(s, axis=0, keepdims=True))
        pe = jnp.exp2(s - m_new).astype(BF16)
        return m_new, jnp.exp2(m_old - m_new) * acc + _dot(vt, pe)

    def tile_step(kt, carries, window, causal):
        k0 = pl.multiple_of(kt * tk, tk)
        kpos = k0 + k_io
        out = []
        for g in groups:
            c_slc, c_swa = carries[g]
            if window:
                keys = jnp.concatenate([ks_ref[0, g, pl.ds(k0, tk), :], kw_ref[0, g, pl.ds(k0, tk), :]], axis=0)
                s2 = _dot(keys, q_aug[g])
                s_slc = s2[:tk]
                if causal:
                    s_slc = s_slc + tile_heads(jnp.where(kpos <= t_io, 0.0, NEG))
                wbias = jnp.where((kpos <= t_io) & (kpos > t_io - WINDOW), 0.0, NEG)
                c_slc = online_step(c_slc, s_slc, vst_ref[0, g, :, pl.ds(k0, tk)])
                c_swa = online_step(c_swa, s2[tk:] + tile_heads(wbias), vwt_ref[0, g, :, pl.ds(k0, tk)])
            else:
                s = _dot(ks_ref[0, g, pl.ds(k0, tk), :], q_aug[g])
                c_slc = online_step(c_slc, s, vst_ref[0, g, :, pl.ds(k0, tk)])
            out.append((c_slc, c_swa))
        return tuple(out)

    carries = tile_step(diag, init, True, True)
    carries = lax.fori_loop(1, n_win, lambda i, c: tile_step(diag - i, c, True, False), carries)
    carries = lax.fori_loop(n_win, diag + 1, lambda i, c: tile_step(diag - i, c, False, False), carries)

    grp_w = NSA_GROUP * NSA_DIM
    for g in groups:
        (_, acc_s), (_, acc_w) = carries[g]
        o_slc = acc_s[:NSA_DIM] / acc_s[NSA_DIM:NSA_DIM + 1]
        o_swa = acc_w[:NSA_DIM] / acc_w[NSA_DIM:NSA_DIM + 1]
        gate = _sigmoid(gate_ref[:, g * LANES:(g + 1) * LANES].T)
        outs = []
        for h in range(NSA_GROUP):
            ls = slice(h * tq, (h + 1) * tq)
            g0 = gate[N_BRANCH * h:N_BRANCH * h + 1, :]
            g1 = gate[N_BRANCH * h + 1:N_BRANCH * h + 2, :]
            g2 = gate[N_BRANCH * h + 2:N_BRANCH * h + 3, :]
            outs.append(g0 * o_cmps[g][:, ls] + g1 * o_slc[:, ls] + g2 * o_swa[:, ls])
        o_ref[:, g * grp_w:(g + 1) * grp_w] = jnp.concatenate(outs, axis=0).T.astype(o_ref.dtype)


def _nsa_attn(qt, kc, vct, ks, vst, kw, vwt, gate, batch, seq):
    tq = min(ATT_Q, seq)
    nq = seq // tq
    nseg = seq // CMP_STRIDE
    nslc = seq // SLC_LEN
    top = min(SLC_TOP, nslc)
    c_start = np.arange(nseg) * CMP_STRIDE
    c_end = c_start + CMP_LEN - 1
    s_start = np.arange(nslc) * SLC_LEN
    s_end = s_start + SLC_LEN - 1
    overlap = (c_start[None, :] <= s_end[:, None]) & (c_end[None, :] >= s_start[:, None])
    overlap[:, nseg - 1] = False
    mcs = jnp.asarray(overlap.astype(np.float32), BF16)
    k_full = pl.BlockSpec((1, NSA_KV_HEADS, seq, K_AUG), lambda b, i: (b, 0, 0, 0))
    vt_full = pl.BlockSpec((1, NSA_KV_HEADS, V_AUG, seq), lambda b, i: (b, 0, 0, 0))
    return pl.pallas_call(
        functools.partial(_nsa_attn_kernel, seq=seq, top=top),
        grid=(batch, nq),
        in_specs=[
            pl.BlockSpec((1, NSA_HEADS, NSA_DIM, tq), lambda b, i: (b, 0, 0, i)),
            pl.BlockSpec((1, NSA_KV_HEADS, nseg, NSA_DIM), lambda b, i: (b, 0, 0, 0)),
            pl.BlockSpec((1, NSA_KV_HEADS, NSA_DIM, nseg), lambda b, i: (b, 0, 0, 0)),
            k_full, vt_full, k_full, vt_full,
            pl.BlockSpec((tq, NSA_KV_HEADS * LANES), lambda b, i: (b * nq + i, 0)),
            pl.BlockSpec((nslc, nseg), lambda b, i: (0, 0)),
        ],
        out_specs=pl.BlockSpec((tq, NSA_WIDTH), lambda b, i: (b * nq + i, 0)),
        out_shape=jax.ShapeDtypeStruct((batch * seq, NSA_WIDTH), BF16),
        compiler_params=_params(("parallel", "arbitrary")),
        name="nsa_attn",
    )(qt, kc, vct, ks, vst, kw, vwt, gate, mcs)


def _out_ffn_kernel(h_ref, ohg_ref, onsa_ref, wout_ref, gain_ref, wg_ref, wu_ref, wo_ref,
                    o_ref, xn_ref, acc_ref):
    j = pl.program_id(1)

    @pl.when(j == 0)
    def _():
        h1 = (h_ref[...] + _dot(ohg_ref[...], wout_ref[:HG_WIDTH, :])
              + _dot(onsa_ref[...], wout_ref[HG_WIDTH:, :]))
        acc_ref[...] = h1
        ms = jnp.mean(h1 * h1, axis=-1, keepdims=True)
        xn_ref[...] = (h1 * lax.rsqrt(ms + RMS_EPS) * gain_ref[...]).astype(BF16)

    xn = xn_ref[...]
    act = (_silu(_dot(xn, wg_ref[...])) * _dot(xn, wu_ref[...])).astype(BF16)
    acc_ref[...] += _dot(act, wo_ref[...])

    @pl.when(j == pl.num_programs(1) - 1)
    def _():
        o_ref[...] = acc_ref[...]


def _out_ffn(h, o_hg, o_nsa, w_out, gain, w_ffn_in, w_ffn_out):
    n = h.shape[0]
    tm = PROJ_ROWS
    nff = D_FF // FF_TILE
    mixw = HG_WIDTH + NSA_WIDTH
    return pl.pallas_call(
        _out_ffn_kernel,
        grid=(n // tm, nff),
        in_specs=[
            pl.BlockSpec((tm, D_MODEL), lambda i, j: (i, 0)),
            pl.BlockSpec((tm, HG_WIDTH), lambda i, j: (i, 0)),
            pl.BlockSpec((tm, NSA_WIDTH), lambda i, j: (i, 0)),
            pl.BlockSpec((mixw, D_MODEL), lambda i, j: (0, 0)),
            pl.BlockSpec((1, D_MODEL), lambda i, j: (0, 0)),
            pl.BlockSpec((D_MODEL, FF_TILE), lambda i, j: (0, j)),
            pl.BlockSpec((D_MODEL, FF_TILE), lambda i, j: (0, nff + j)),
            pl.BlockSpec((FF_TILE, D_MODEL), lambda i, j: (j, 0)),
        ],
        out_specs=pl.BlockSpec((tm, D_MODEL), lambda i, j: (i, 0)),
        out_shape=jax.ShapeDtypeStruct((n, D_MODEL), F32),
        scratch_shapes=[pltpu.VMEM((tm, D_MODEL), BF16), pltpu.VMEM((tm, D_MODEL), F32)],
        compiler_params=_params(("parallel", "arbitrary")),
        name="out_ffn",
    )(h, o_hg, o_nsa, w_out, gain, w_ffn_in, w_ffn_in, w_ffn_out)


def _pad_w_in(w):
    pieces = [w[:, :_KV_END]]
    for g in range(NSA_KV_HEADS):
        cols = w[:, _KV_END + g * GATES_PER_GROUP:_KV_END + (g + 1) * GATES_PER_GROUP]
        pieces.append(jnp.pad(cols, ((0, 0), (0, LANES - GATES_PER_GROUP))))
    return jnp.concatenate(pieces, axis=1).astype(BF16)


def kernel(x, w_in, w_out, hg_lb_logits, hg_gnorm, q_gain, k_gain, cmp_pe_k, cmp_w1_k, cmp_w2_k,
           cmp_pe_v, cmp_w1_v, cmp_w2_v, w_ffn_in, w_ffn_out, norm_mix, norm_ffn):
    batch, seq, _ = x.shape
    depth = w_in.shape[0]
    h = x.reshape(batch * seq, D_MODEL)
    for l in range(depth):
        hgp, nq, kc_tok, vc_tok, kv4, gate = _norm_proj(h, norm_mix[l][None, :], _pad_w_in(w_in[l]))
        o_hg = _hgrn(hgp, hg_lb_logits, hg_gnorm[l][None, :], l, batch, seq)
        qt, ks, vst, kw, vwt = _nsa_prep(nq, kv4, q_gain[l], k_gain[l], batch, seq)
        kc, vct = _nsa_cmp(kc_tok, vc_tok, cmp_pe_k[l], cmp_w1_k[l], cmp_w2_k[l],
                           cmp_pe_v[l], cmp_w1_v[l], cmp_w2_v[l], k_gain[l, 0], batch, seq)
        o_nsa = _nsa_attn(qt, kc, vct, ks, vst, kw, vwt, gate, batch, seq)
        h = _out_ffn(h, o_hg, o_nsa, w_out[l].astype(BF16), norm_ffn[l][None, :],
                     w_ffn_in[l].astype(BF16), w_ffn_out[l].astype(BF16))
    return h.reshape(batch, seq, D_MODEL)
```

```python
import functools

import jax
import jax.numpy as jnp
import numpy as np
from jax import lax
from jax.experimental import pallas as pl
from jax.experimental.pallas import tpu as pltpu

F32 = jnp.float32
BF16 = jnp.bfloat16

D_MODEL = 1024
HG_HEADS = 4
HG_DIM = 128
HG_WIDTH = HG_HEADS * HG_DIM
NSA_HEADS = 8
NSA_KV_HEADS = 2
NSA_GROUP = NSA_HEADS // NSA_KV_HEADS
NSA_DIM = 64
NSA_WIDTH = NSA_HEADS * NSA_DIM
NSA_KV_WIDTH = NSA_KV_HEADS * NSA_DIM
CMP_LEN = 32
CMP_STRIDE = 16
CMP_HIDDEN = 256
SLC_LEN = 64
SLC_TOP = 16
WINDOW = 512
N_BRANCH = 3
D_FF = 2816
ROPE_THETA = 10000.0
RMS_EPS = 1e-6
NEG = -1e30
FORCED_SCORE = 1e9
LOG2_E = 1.4426950408889634

LANES = 128
V7X_VMEM_LIMIT = 56 * 1024 * 1024

PROJ_ROWS = 512
HG_ROWS = 256
HG_CHUNK = 32
HG_SAFE_DECAY = 60.0
ATT_Q = 256
ATT_K = 256
FF_TILE = 1408
K_AUG = 2 * NSA_DIM
V_AUG = NSA_DIM + 16

_HG_END = 4 * HG_WIDTH
_NQ_END = _HG_END + NSA_WIDTH
_KC_END = _NQ_END + NSA_KV_WIDTH
_VC_END = _KC_END + NSA_KV_WIDTH
_KV_END = _VC_END + 4 * NSA_KV_WIDTH
_GATE_END = _KV_END + NSA_KV_HEADS * LANES
GATES_PER_GROUP = NSA_GROUP * N_BRANCH


def _dot(a, b):
    return jnp.dot(a, b, preferred_element_type=F32)


def _dot_nt(a, b):
    return lax.dot_general(a, b, (((1,), (1,)), ((), ())), preferred_element_type=F32)


def _split3(x):
    hi = x.astype(BF16)
    r1 = x - hi.astype(F32)
    mid = r1.astype(BF16)
    lo = (r1 - mid.astype(F32)).astype(BF16)
    return hi, mid, lo


def _sigmoid(x):
    return 1.0 / (1.0 + jnp.exp(-x))


def _silu(x):
    return x * _sigmoid(x)


def _params(semantics):
    return pltpu.CompilerParams(dimension_semantics=semantics, vmem_limit_bytes=V7X_VMEM_LIMIT)


def _norm_proj_kernel(x_ref, gain_ref, w_ref, qg_ref, kg_ref, cos_ref, sin_ref, bd_ref,
                      hg_ref, kc_ref, vc_ref, gate_ref, qt_out, ks_out, vst_out, kw_out, vwt_out):
    x = x_ref[...]
    ms = jnp.mean(x * x, axis=-1, keepdims=True)
    xn = (x * lax.rsqrt(ms + RMS_EPS) * gain_ref[...]).astype(BF16)
    hg_ref[...] = _dot(xn, w_ref[:, :_HG_END])
    kc_ref[...] = _dot(xn, w_ref[:, _NQ_END:_KC_END])
    vc_ref[...] = _dot(xn, w_ref[:, _KC_END:_VC_END])
    gate_ref[...] = _dot(xn, w_ref[:, _KV_END:_GATE_END])
    _nsa_prep(_dot(xn, w_ref[:, _HG_END:_NQ_END]), _dot(xn, w_ref[:, _VC_END:_KV_END]),
              qg_ref, kg_ref, cos_ref, sin_ref, bd_ref, qt_out, ks_out, vst_out, kw_out, vwt_out)


def _norm_proj(h, gain, w_pad, q_gain, k_gain, batch, seq):
    n = h.shape[0]
    rows = min(PROJ_ROWS, seq)
    nt = seq // rows
    assert seq // SLC_LEN <= NSA_DIM, "selection one-hot must fit the spare key lanes"
    cos_t, sin_t = _rope_tables(jnp.arange(seq), NSA_HEADS)
    gid = np.arange(NSA_WIDTH) // NSA_DIM
    bd = jnp.asarray((gid[:, None] == gid[None, :]).astype(np.float32), BF16)
    qg = jnp.tile(q_gain, NSA_HEADS)[None, :]
    kg = jnp.stack([jnp.tile(k_gain[1], NSA_KV_HEADS), jnp.tile(k_gain[2], NSA_KV_HEADS)])
    widths = (_HG_END, NSA_KV_WIDTH, NSA_KV_WIDTH, NSA_KV_HEADS * LANES)
    k_shape = jax.ShapeDtypeStruct((batch, NSA_KV_HEADS, seq, K_AUG), BF16)
    k_spec = pl.BlockSpec((1, NSA_KV_HEADS, rows, K_AUG), lambda b, c: (b, 0, c, 0))
    vt_shape = jax.ShapeDtypeStruct((batch, NSA_KV_HEADS, V_AUG, seq), BF16)
    vt_spec = pl.BlockSpec((1, NSA_KV_HEADS, V_AUG, rows), lambda b, c: (b, 0, 0, c))

    def const(shape):
        return pl.BlockSpec(shape, lambda b, c: (0,) * len(shape))

    return pl.pallas_call(
        _norm_proj_kernel,
        grid=(batch, nt),
        in_specs=[
            pl.BlockSpec((rows, D_MODEL), lambda b, c: (b * nt + c, 0)),
            const((1, D_MODEL)),
            const((D_MODEL, _GATE_END)),
            const((1, NSA_WIDTH)),
            const((2, NSA_KV_WIDTH)),
            pl.BlockSpec((rows, NSA_WIDTH), lambda b, c: (c, 0)),
            pl.BlockSpec((rows, NSA_WIDTH), lambda b, c: (c, 0)),
            const((NSA_WIDTH, NSA_WIDTH)),
        ],
        out_specs=[pl.BlockSpec((rows, w), lambda b, c: (b * nt + c, 0)) for w in widths] + [
            pl.BlockSpec((1, NSA_HEADS, NSA_DIM, rows), lambda b, c: (b, 0, 0, c)),
            k_spec, vt_spec, k_spec, vt_spec,
        ],
        out_shape=[jax.ShapeDtypeStruct((n, w), F32) for w in widths] + [
            jax.ShapeDtypeStruct((batch, NSA_HEADS, NSA_DIM, seq), BF16),
            k_shape, vt_shape, k_shape, vt_shape,
        ],
        compiler_params=_params(("parallel", "parallel")),
        name="norm_proj",
    )(h, gain, w_pad, qg, kg, cos_t, sin_t, bd)


def _hgrn_kernel(q_ref, f_ref, i_ref, g_ref, lbl_ref, gn_ref, tri_ref, o_ref,
                 s_ref, b_ref, k_ref, *, layer, depth):
    rows = q_ref.shape[0]
    n_chunks = rows // HG_CHUNK
    c = HG_CHUNK

    @pl.when(pl.program_id(1) == 0)
    def _():
        s_ref[...] = jnp.zeros_like(s_ref)

    lg = lbl_ref[...]
    rows_l = [lg[i:i + 1, :] for i in range(depth)]
    mx = functools.reduce(jnp.maximum, rows_l)
    ex = [jnp.exp(r - mx) for r in rows_l]
    den = functools.reduce(lambda a, b_: a + b_, ex)
    pr = [e / den for e in ex]
    lb = functools.reduce(lambda a, b_: a + b_, pr[:layer + 1]) - pr[0]

    f = lb + (1.0 - lb) * _sigmoid(f_ref[...])
    kk = 1.0 - f
    hi, mid, lo = _split3(jnp.log(f))
    tri = tri_ref[...]
    b_all = _dot(tri, hi) + _dot(tri, mid) + _dot(tri, lo)
    needs_exact = jnp.min(b_all) < -HG_SAFE_DECAY
    gn = gn_ref[...]

    def finish(o, rows_sl, cs):
        ms = jnp.mean(o * o, axis=-1, keepdims=True)
        y = o * lax.rsqrt(ms + RMS_EPS) * gn
        o_ref[rows_sl, cs] = (y * _silu(g_ref[rows_sl, cs])).astype(o_ref.dtype)

    @pl.when(jnp.logical_not(needs_exact))
    def _():
        qf = _silu(q_ref[...])
        qt_l, kt_l, kh_l, qi_l, kb_l, dec_l = [], [], [], [], [], []
        for j in range(n_chunks // 2):
            parts = []
            for half in range(2):
                rs = slice((2 * j + half) * c, (2 * j + half + 1) * c)
                b = b_all[rs]
                tot = b[c - 1:c, :]
                parts.append((qf[rs] * jnp.exp(b), kk[rs] * jnp.exp(-b), kk[rs] * jnp.exp(tot - b), jnp.exp(tot)))
            (qa, ka, ha, ea), (qb, kb, hb, eb) = parts
            qt_l += [qa, qb]
            kt_l += [ka, kb]
            kh_l += [ha, hb]
            qi_l += [qa, qb * ea]
            kb_l += [ha * eb, hb]
            dec_l.append(ea * eb)
        qt = jnp.concatenate(qt_l, axis=0).astype(BF16)
        kt = jnp.concatenate(kt_l, axis=0).astype(BF16)
        kh = jnp.concatenate(kh_l, axis=0).astype(BF16)
        qi = jnp.concatenate(qi_l, axis=0).astype(BF16)
        kb_all = jnp.concatenate(kb_l, axis=0).astype(BF16)
        t_io = lax.broadcasted_iota(jnp.int32, (rows, rows), 0)
        s_io = lax.broadcasted_iota(jnp.int32, (rows, rows), 1)
        same_chunk = (t_io // c == s_io // c) & (s_io <= t_io)
        cross = (t_io // (2 * c) == s_io // (2 * c)) & (t_io // c > s_io // c)
        for h in range(HG_HEADS):
            cs = slice(h * HG_DIM, (h + 1) * HG_DIM)
            v = i_ref[:, cs]
            a = (jnp.where(same_chunk, _dot_nt(qt[:, cs], kt[:, cs]), 0.0)
                 + jnp.where(cross, _dot_nt(qt[:, cs], kh[:, cs]), 0.0))
            o_intra = _dot(a.astype(BF16), v.astype(BF16))
            st = s_ref[h]
            for j in range(n_chunks // 2):
                rs = slice(2 * j * c, (2 * j + 2) * c)
                o = o_intra[rs] + _dot_nt(qi[rs, cs], st.astype(BF16))
                st = st * dec_l[j][:, cs] + _dot(v[rs].T.astype(BF16), kb_all[rs, cs])
                finish(o, rs, cs)
            s_ref[h] = st

    @pl.when(needs_exact)
    def _():
        b_ref[...] = b_all
        k_ref[...] = kk
        t_io = lax.broadcasted_iota(jnp.int32, (c, c), 0)
        s_io = lax.broadcasted_iota(jnp.int32, (c, c), 1)
        row_io = lax.broadcasted_iota(jnp.int32, (c, HG_DIM), 0)

        def chunk_step(ci, _):
            r0 = pl.multiple_of(ci * c, c)
            for h in range(HG_HEADS):
                cs = slice(h * HG_DIM, (h + 1) * HG_DIM)
                qf = _silu(q_ref[pl.ds(r0, c), cs])
                b = b_ref[pl.ds(r0, c), cs]
                k = k_ref[pl.ds(r0, c), cs]
                v = i_ref[pl.ds(r0, c), cs]
                b_last = b[c - 1:c, :]
                a = jnp.zeros((c, c), F32)
                for s in range(c):
                    d = jnp.where(row_io >= s, b - b[s:s + 1, :], NEG)
                    col = jnp.sum(qf * k[s:s + 1, :] * jnp.exp(d), axis=-1, keepdims=True)
                    a = jnp.where(s_io == s, col, a)
                st = s_ref[h]
                o = _dot(a.astype(BF16), v.astype(BF16)) + _dot_nt((qf * jnp.exp(b)).astype(BF16), st.astype(BF16))
                s_ref[h] = st * jnp.exp(b_last) + _dot(v.T.astype(BF16), (k * jnp.exp(b_last - b)).astype(BF16))
                finish(o, pl.ds(r0, c), cs)

        lax.fori_loop(0, n_chunks, chunk_step, None)


def _hgrn(hgp, lb_logits, gnorm, layer, batch, seq):
    n = hgp.shape[0]
    rows = min(HG_ROWS, seq)
    nt = seq // rows
    depth = lb_logits.shape[0]
    idx = np.arange(rows)
    tri = ((idx[:, None] >= idx[None, :]) & (idx[:, None] // HG_CHUNK == idx[None, :] // HG_CHUNK))
    tri = jnp.asarray(tri.astype(np.float32), BF16)

    def col_spec(j):
        return pl.BlockSpec((rows, HG_WIDTH), lambda b, c, j=j: (b * nt + c, j))

    return pl.pallas_call(
        functools.partial(_hgrn_kernel, layer=layer, depth=depth),
        grid=(batch, nt),
        in_specs=[
            col_spec(0), col_spec(1), col_spec(2), col_spec(3),
            pl.BlockSpec((depth, HG_WIDTH), lambda b, c: (0, 0)),
            pl.BlockSpec((1, HG_DIM), lambda b, c: (0, 0)),
            pl.BlockSpec((rows, rows), lambda b, c: (0, 0)),
        ],
        out_specs=pl.BlockSpec((rows, HG_WIDTH), lambda b, c: (b * nt + c, 0)),
        out_shape=jax.ShapeDtypeStruct((n, HG_WIDTH), BF16),
        scratch_shapes=[
            pltpu.VMEM((HG_HEADS, HG_DIM, HG_DIM), F32),
            pltpu.VMEM((rows, HG_WIDTH), F32),
            pltpu.VMEM((rows, HG_WIDTH), F32),
        ],
        compiler_params=_params(("parallel", "arbitrary")),
        name="hgrn",
    )(hgp, hgp, hgp, hgp, lb_logits, gnorm, tri)


def _group_rms(x, bd, gain):
    sq = x * x
    hi = sq.astype(BF16)
    lo = (sq - hi.astype(F32)).astype(BF16)
    ss = _dot(hi, bd) + _dot(lo, bd)
    return x * lax.rsqrt(ss * (1.0 / NSA_DIM) + RMS_EPS) * gain


def _rope_lanes(x, cos, sin_signed):
    w = x.shape[-1]
    half = NSA_DIM // 2
    lane = lax.broadcasted_iota(jnp.int32, x.shape, 1)
    first = (lane & (NSA_DIM - 1)) < half
    swapped = jnp.where(first, pltpu.roll(x, w - half, 1), pltpu.roll(x, half, 1))
    return x * cos + swapped * sin_signed


def _nsa_prep(nq, kv, qg_ref, kg_ref, cos_ref, sin_ref, bd_ref, qt_out, ks_out, vst_out, kw_out, vwt_out):
    cos = cos_ref[...]
    sin = sin_ref[...]
    bd = bd_ref[...]
    kvw = NSA_KV_WIDTH
    rows = nq.shape[0]
    q = _rope_lanes(_group_rms(nq, bd, qg_ref[...]), cos, sin) * (NSA_DIM ** -0.5 * LOG2_E)
    qt_out[0] = q.T.reshape(NSA_HEADS, NSA_DIM, rows).astype(BF16)
    bdk = bd[:kvw, :kvw]
    cosk = cos[:, :kvw]
    sink = sin[:, :kvw]
    ks = _rope_lanes(_group_rms(kv[:, 0:kvw], bdk, kg_ref[0:1, :]), cosk, sink)
    kw = _rope_lanes(_group_rms(kv[:, 2 * kvw:3 * kvw], bdk, kg_ref[1:2, :]), cosk, sink)
    pad_rows = V_AUG - NSA_DIM
    ones_row = jnp.where(lax.broadcasted_iota(jnp.int32, (NSA_KV_HEADS, pad_rows, rows), 1) == 0, 1.0, 0.0)

    def values_t(v):
        vt = v.T.reshape(NSA_KV_HEADS, NSA_DIM, rows)
        return jnp.concatenate([vt, ones_row], axis=1).astype(BF16)

    vst_out[0] = values_t(kv[:, kvw:2 * kvw])
    vwt_out[0] = values_t(kv[:, 3 * kvw:4 * kvw])
    tok = pl.program_id(1) * rows + lax.broadcasted_iota(jnp.int32, (rows, NSA_DIM), 0)
    onehot = jnp.where(lax.broadcasted_iota(jnp.int32, (rows, NSA_DIM), 1) == tok // SLC_LEN, 1.0, 0.0)
    zeros = jnp.zeros((rows, NSA_DIM), F32)
    for g in range(NSA_KV_HEADS):
        sl = slice(g * NSA_DIM, (g + 1) * NSA_DIM)
        ks_out[0, g] = jnp.concatenate([ks[:, sl], onehot], axis=1).astype(BF16)
        kw_out[0, g] = jnp.concatenate([kw[:, sl], zeros], axis=1).astype(BF16)


def _rope_tables(pos, reps):
    half = NSA_DIM // 2
    freqs = ROPE_THETA ** (-jnp.arange(half, dtype=F32) / half)
    ang = pos.astype(F32)[:, None] * freqs[None, :]
    cos = jnp.cos(ang)
    sin = jnp.sin(ang)
    cos_t = jnp.tile(jnp.concatenate([cos, cos], axis=1), (1, reps))
    sin_t = jnp.tile(jnp.concatenate([-sin, sin], axis=1), (1, reps))
    return cos_t, sin_t


def _nsa_cmp_kernel(kseg_ref, vseg_ref, pek_ref, pev_ref, w1k_ref, w1v_ref, w2k_ref, w2v_ref,
                    kg_ref, cos_ref, sin_ref, kc_out, vct_out):
    nseg = kseg_ref.shape[1]

    def hidden(seg, pe_ref, w1_ref):
        y0 = _dot((seg + pe_ref[0:1, :]).astype(BF16), w1_ref[0])
        y1 = _dot((seg + pe_ref[1:2, :]).astype(BF16), w1_ref[1])
        act = _silu(y0 + pltpu.roll(y1, nseg - 1, 0)).astype(BF16)
        return [act[:, g * CMP_HIDDEN:(g + 1) * CMP_HIDDEN] for g in range(NSA_KV_HEADS)]

    hk = hidden(kseg_ref[0], pek_ref, w1k_ref)
    hv = hidden(vseg_ref[0], pev_ref, w1v_ref)
    half = NSA_DIM // 2
    for g in range(NSA_KV_HEADS):
        x = _dot(hk[g], w2k_ref[...])
        ms = jnp.mean(x * x, axis=-1, keepdims=True)
        xn = x * lax.rsqrt(ms + RMS_EPS) * kg_ref[...]
        swapped = jnp.concatenate([xn[:, half:], xn[:, :half]], axis=1)
        kc_out[0, g] = (xn * cos_ref[...] + swapped * sin_ref[...]).astype(BF16)
        vct_out[0, g] = _dot_nt(w2v_ref[...], hv[g]).astype(BF16)


def _cmp_weights(pe, w1):
    pe_r = pe.reshape(2, CMP_STRIDE, 1, NSA_DIM)
    pe_seg = jnp.broadcast_to(pe_r, (2, CMP_STRIDE, NSA_KV_HEADS, NSA_DIM)).reshape(2, CMP_STRIDE * NSA_KV_WIDTH)
    w1r = w1.reshape(2, CMP_STRIDE, NSA_DIM, CMP_HIDDEN)
    eye = jnp.eye(NSA_KV_HEADS, dtype=w1.dtype)
    wbig = jnp.einsum('rldj,gh->rlgdhj', w1r, eye)
    wbig = wbig.reshape(2, CMP_STRIDE * NSA_KV_WIDTH, NSA_KV_HEADS * CMP_HIDDEN)
    return pe_seg, wbig.astype(BF16)


def _nsa_cmp(kc_tok, vc_tok, pe_k, w1_k, w2_k, pe_v, w1_v, w2_v, k_gain0, batch, seq):
    nseg = seq // CMP_STRIDE
    segw = CMP_STRIDE * NSA_KV_WIDTH
    kseg = kc_tok.reshape(batch, nseg, segw)
    vseg = vc_tok.reshape(batch, nseg, segw)
    pek, w1k = _cmp_weights(pe_k, w1_k)
    pev, w1v = _cmp_weights(pe_v, w1_v)
    cmp_end = jnp.arange(nseg) * CMP_STRIDE + CMP_LEN - 1
    cos_t, sin_t = _rope_tables(cmp_end, 1)
    hid2 = NSA_KV_HEADS * CMP_HIDDEN
    seg_spec = pl.BlockSpec((1, nseg, segw), lambda b: (b, 0, 0))
    out_spec = pl.BlockSpec((1, NSA_KV_HEADS, nseg, NSA_DIM), lambda b: (b, 0, 0, 0))
    out_shape = jax.ShapeDtypeStruct((batch, NSA_KV_HEADS, nseg, NSA_DIM), BF16)
    out_t_spec = pl.BlockSpec((1, NSA_KV_HEADS, NSA_DIM, nseg), lambda b: (b, 0, 0, 0))
    out_t_shape = jax.ShapeDtypeStruct((batch, NSA_KV_HEADS, NSA_DIM, nseg), BF16)
    return pl.pallas_call(
        _nsa_cmp_kernel,
        grid=(batch,),
        in_specs=[
            seg_spec, seg_spec,
            pl.BlockSpec((2, segw), lambda b: (0, 0)),
            pl.BlockSpec((2, segw), lambda b: (0, 0)),
            pl.BlockSpec((2, segw, hid2), lambda b: (0, 0, 0)),
            pl.BlockSpec((2, segw, hid2), lambda b: (0, 0, 0)),
            pl.BlockSpec((CMP_HIDDEN, NSA_DIM), lambda b: (0, 0)),
            pl.BlockSpec((NSA_DIM, CMP_HIDDEN), lambda b: (0, 0)),
            pl.BlockSpec((1, NSA_DIM), lambda b: (0, 0)),
            pl.BlockSpec((nseg, NSA_DIM), lambda b: (0, 0)),
            pl.BlockSpec((nseg, NSA_DIM), lambda b: (0, 0)),
        ],
        out_specs=[out_spec, out_t_spec],
        out_shape=[out_shape, out_t_shape],
        compiler_params=_params(("parallel",)),
        name="nsa_cmp",
    )(kseg, vseg, pek, pev, w1k, w1v, w2_k.astype(BF16), w2_v.T.astype(BF16),
      k_gain0[None, :], cos_t, sin_t)


def _nsa_attn_kernel(qt_ref, kc_ref, vct_ref, ks_ref, vst_ref, kw_ref, vwt_ref, gate_ref,
                     mcs_ref, o_ref, *, seq, top):
    tq = qt_ref.shape[3]
    tk = min(ATT_K, seq)
    nseg = kc_ref.shape[2]
    nslc = mcs_ref.shape[0]
    t0 = pl.program_id(1) * tq
    groups = range(NSA_KV_HEADS)
    qts = [jnp.concatenate([qt_ref[0, g * NSA_GROUP + h] for h in range(NSA_GROUP)], axis=1) for g in groups]

    def tile_heads(x):
        return jnp.concatenate([x] * NSA_GROUP, axis=1)

    blk_n = lax.broadcasted_iota(jnp.int32, (nseg, tq), 0)
    t_col = t0 + lax.broadcasted_iota(jnp.int32, (nseg, tq), 1)
    valid = (blk_n * CMP_STRIDE + (CMP_LEN - 1)) <= t_col
    j_io = lax.broadcasted_iota(jnp.int32, (nslc, tq), 0)
    cur = (t0 + lax.broadcasted_iota(jnp.int32, (nslc, tq), 1)) // SLC_LEN
    forced = (j_io == 0) | (j_io == cur) | (j_io == cur - 1)
    mcs = mcs_ref[...]
    o_cmps = []
    q_aug = []
    for g in groups:
        sc = _dot(kc_ref[0, g], qts[g])
        p_heads = []
        for h in range(NSA_GROUP):
            sm = jnp.where(valid, sc[:, h * tq:(h + 1) * tq], NEG)
            p = jnp.where(valid, jnp.exp2(sm - jnp.max(sm, axis=0, keepdims=True)), 0.0)
            p_heads.append(p / jnp.maximum(jnp.sum(p, axis=0, keepdims=True), 1e-30))
        o_cmps.append(_dot(vct_ref[0, g], jnp.concatenate(p_heads, axis=1).astype(BF16)))

        p_sum = functools.reduce(lambda a, b: a + b, p_heads)
        imp = functools.reduce(lambda a, b: a + b, [_dot(mcs, piece) for piece in _split3(p_sum)])
        score = jnp.where(forced, FORCED_SCORE, jnp.where(j_io <= cur, imp, NEG))
        rank = jnp.zeros((nslc, tq), F32)
        for jp in range(nslc):
            other = score[jp:jp + 1, :]
            ahead = (other > score) | ((other == score) & (j_io > jp))
            rank = rank + jnp.where(ahead, 1.0, 0.0)
        sel_bias = jnp.where(rank < top, 0.0, NEG)
        sel_rows = jnp.concatenate([sel_bias, jnp.zeros((K_AUG - NSA_DIM - nslc, tq), F32)], axis=0)
        q_aug.append(jnp.concatenate([qts[g], tile_heads(sel_rows).astype(BF16)], axis=0))

    lanes = NSA_GROUP * tq
    init1 = (jnp.full((1, lanes), NEG, F32), jnp.zeros((V_AUG, lanes), F32))
    init = tuple((init1, init1) for _ in groups)
    k_io = lax.broadcasted_iota(jnp.int32, (tk, tq), 0)
    t_io = t0 + lax.broadcasted_iota(jnp.int32, (tk, tq), 1)
    diag = t0 // tk
    n_win = -(-(WINDOW - 1) // tk) + 1

    def online_step(carry, s, vt):
        m_old, acc = carry
        m_new = jnp.maximum(m_old, jnp.max(s, axis=0, keepdims=True))
        pe = jnp.exp2(s - m_new).astype(BF16)
        return m_new, jnp.exp2(m_old - m_new) * acc + _dot(vt, pe)

    def tile_step(kt, carries, window, causal=False, maybe_absent=False):
        k0 = pl.multiple_of((jnp.maximum(kt, 0) if maybe_absent else kt) * tk, tk)
        kpos = kt * tk + k_io
        out = []
        for g in groups:
            c_slc, c_swa = carries[g]
            if window:
                keys = jnp.concatenate([ks_ref[0, g, pl.ds(k0, tk), :], kw_ref[0, g, pl.ds(k0, tk), :]], axis=0)
                s2 = _dot(keys, q_aug[g])
                s_slc = s2[:tk]
                if causal:
                    s_slc = s_slc + tile_heads(jnp.where(kpos <= t_io, 0.0, NEG))
                in_win = (kpos <= t_io) & (kpos > t_io - WINDOW)
                if maybe_absent:
                    s_slc = s_slc + jnp.where(kt >= 0, 0.0, NEG)
                    in_win = in_win & (kpos >= 0)
                wbias = jnp.where(in_win, 0.0, NEG)
                c_slc = online_step(c_slc, s_slc, vst_ref[0, g, :, pl.ds(k0, tk)])
                c_swa = online_step(c_swa, s2[tk:] + tile_heads(wbias), vwt_ref[0, g, :, pl.ds(k0, tk)])
            else:
                s = _dot(ks_ref[0, g, pl.ds(k0, tk), :], q_aug[g])
                c_slc = online_step(c_slc, s, vst_ref[0, g, :, pl.ds(k0, tk)])
            out.append((c_slc, c_swa))
        return tuple(out)

    carries = tile_step(diag, init, True, causal=True)
    for i in range(1, n_win):
        carries = tile_step(diag - i, carries, True, maybe_absent=True)
    rest = jnp.maximum(diag - (n_win - 1), 0)

    def tile_pair(p, c):
        kt = rest - 1 - 2 * p
        return tile_step(kt - 1, tile_step(kt, c, False), False)

    carries = lax.fori_loop(0, rest // 2, tile_pair, carries)
    carries = lax.cond(rest % 2 == 1, lambda c: tile_step(0, c, False), lambda c: c, carries)

    grp_w = NSA_GROUP * NSA_DIM
    for g in groups:
        (_, acc_s), (_, acc_w) = carries[g]
        o_slc = acc_s[:NSA_DIM] / acc_s[NSA_DIM:NSA_DIM + 1]
        o_swa = acc_w[:NSA_DIM] / acc_w[NSA_DIM:NSA_DIM + 1]
        gate = _sigmoid(gate_ref[:, g * LANES:(g + 1) * LANES].T)
        outs = []
        for h in range(NSA_GROUP):
            ls = slice(h * tq, (h + 1) * tq)
            g0 = gate[N_BRANCH * h:N_BRANCH * h + 1, :]
            g1 = gate[N_BRANCH * h + 1:N_BRANCH * h + 2, :]
            g2 = gate[N_BRANCH * h + 2:N_BRANCH * h + 3, :]
            outs.append(g0 * o_cmps[g][:, ls] + g1 * o_slc[:, ls] + g2 * o_swa[:, ls])
        o_ref[:, g * grp_w:(g + 1) * grp_w] = jnp.concatenate(outs, axis=0).T.astype(o_ref.dtype)


def _nsa_attn(qt, kc, vct, ks, vst, kw, vwt, gate, batch, seq):
    tq = min(ATT_Q, seq)
    nq = seq // tq
    nseg = seq // CMP_STRIDE
    nslc = seq // SLC_LEN
    top = min(SLC_TOP, nslc)
    c_start = np.arange(nseg) * CMP_STRIDE
    c_end = c_start + CMP_LEN - 1
    s_start = np.arange(nslc) * SLC_LEN
    s_end = s_start + SLC_LEN - 1
    overlap = (c_start[None, :] <= s_end[:, None]) & (c_end[None, :] >= s_start[:, None])
    overlap[:, nseg - 1] = False
    mcs = jnp.asarray(overlap.astype(np.float32), BF16)
    k_full = pl.BlockSpec((1, NSA_KV_HEADS, seq, K_AUG), lambda b, i: (b, 0, 0, 0))
    vt_full = pl.BlockSpec((1, NSA_KV_HEADS, V_AUG, seq), lambda b, i: (b, 0, 0, 0))
    return pl.pallas_call(
        functools.partial(_nsa_attn_kernel, seq=seq, top=top),
        grid=(batch, nq),
        in_specs=[
            pl.BlockSpec((1, NSA_HEADS, NSA_DIM, tq), lambda b, i: (b, 0, 0, i)),
            pl.BlockSpec((1, NSA_KV_HEADS, nseg, NSA_DIM), lambda b, i: (b, 0, 0, 0)),
            pl.BlockSpec((1, NSA_KV_HEADS, NSA_DIM, nseg), lambda b, i: (b, 0, 0, 0)),
            k_full, vt_full, k_full, vt_full,
            pl.BlockSpec((tq, NSA_KV_HEADS * LANES), lambda b, i: (b * nq + i, 0)),
            pl.BlockSpec((nslc, nseg), lambda b, i: (0, 0)),
        ],
        out_specs=pl.BlockSpec((tq, NSA_WIDTH), lambda b, i: (b * nq + i, 0)),
        out_shape=jax.ShapeDtypeStruct((batch * seq, NSA_WIDTH), BF16),
        compiler_params=_params(("parallel", "arbitrary")),
        name="nsa_attn",
    )(qt, kc, vct, ks, vst, kw, vwt, gate, mcs)


def _out_ffn_kernel(h_ref, ohg_ref, onsa_ref, wout_ref, gain_ref, wg_ref, wu_ref, wo_ref,
                    o_ref, xn_ref, acc_ref):
    j = pl.program_id(1)

    @pl.when(j == 0)
    def _():
        h1 = (h_ref[...] + _dot(ohg_ref[...], wout_ref[:HG_WIDTH, :])
              + _dot(onsa_ref[...], wout_ref[HG_WIDTH:, :]))
        acc_ref[...] = h1
        ms = jnp.mean(h1 * h1, axis=-1, keepdims=True)
        xn_ref[...] = (h1 * lax.rsqrt(ms + RMS_EPS) * gain_ref[...]).astype(BF16)

    xn = xn_ref[...]
    act = (_silu(_dot(xn, wg_ref[...])) * _dot(xn, wu_ref[...])).astype(BF16)
    acc_ref[...] += _dot(act, wo_ref[...])

    @pl.when(j == pl.num_programs(1) - 1)
    def _():
        o_ref[...] = acc_ref[...]


def _out_ffn(h, o_hg, o_nsa, w_out, gain, w_ffn_in, w_ffn_out):
    n = h.shape[0]
    tm = PROJ_ROWS
    nff = D_FF // FF_TILE
    mixw = HG_WIDTH + NSA_WIDTH
    return pl.pallas_call(
        _out_ffn_kernel,
        grid=(n // tm, nff),
        in_specs=[
            pl.BlockSpec((tm, D_MODEL), lambda i, j: (i, 0)),
            pl.BlockSpec((tm, HG_WIDTH), lambda i, j: (i, 0)),
            pl.BlockSpec((tm, NSA_WIDTH), lambda i, j: (i, 0)),
            pl.BlockSpec((mixw, D_MODEL), lambda i, j: (0, 0)),
            pl.BlockSpec((1, D_MODEL), lambda i, j: (0, 0)),
            pl.BlockSpec((D_MODEL, FF_TILE), lambda i, j: (0, j)),
            pl.BlockSpec((D_MODEL, FF_TILE), lambda i, j: (0, nff + j)),
            pl.BlockSpec((FF_TILE, D_MODEL), lambda i, j: (j, 0)),
        ],
        out_specs=pl.BlockSpec((tm, D_MODEL), lambda i, j: (i, 0)),
        out_shape=jax.ShapeDtypeStruct((n, D_MODEL), F32),
        scratch_shapes=[pltpu.VMEM((tm, D_MODEL), BF16), pltpu.VMEM((tm, D_MODEL), F32)],
        compiler_params=_params(("parallel", "arbitrary")),
        name="out_ffn",
    )(h, o_hg, o_nsa, w_out, gain, w_ffn_in, w_ffn_in, w_ffn_out)


def _pad_w_in(w):
    pieces = [w[:, :_KV_END]]
    for g in range(NSA_KV_HEADS):
        cols = w[:, _KV_END + g * GATES_PER_GROUP:_KV_END + (g + 1) * GATES_PER_GROUP]
        pieces.append(jnp.pad(cols, ((0, 0), (0, LANES - GATES_PER_GROUP))))
    return jnp.concatenate(pieces, axis=1).astype(BF16)


def kernel(x, w_in, w_out, hg_lb_logits, hg_gnorm, q_gain, k_gain, cmp_pe_k, cmp_w1_k, cmp_w2_k,
           cmp_pe_v, cmp_w1_v, cmp_w2_v, w_ffn_in, w_ffn_out, norm_mix, norm_ffn):
    batch, seq, _ = x.shape
    depth = w_in.shape[0]
    h = x.reshape(batch * seq, D_MODEL)
    for l in range(depth):
        hgp, kc_tok, vc_tok, gate, qt, ks, vst, kw, vwt = _norm_proj(
            h, norm_mix[l][None, :], _pad_w_in(w_in[l]), q_gain[l], k_gain[l], batch, seq)
        o_hg = _hgrn(hgp, hg_lb_logits, hg_gnorm[l][None, :], l, batch, seq)
        kc, vct = _nsa_cmp(kc_tok, vc_tok, cmp_pe_k[l], cmp_w1_k[l], cmp_w2_k[l],
                           cmp_pe_v[l], cmp_w1_v[l], cmp_w2_v[l], k_gain[l, 0], batch, seq)
        o_nsa = _nsa_attn(qt, kc, vct, ks, vst, kw, vwt, gate, batch, seq)
        h = _out_ffn(h, o_hg, o_nsa, w_out[l].astype(BF16), norm_ffn[l][None, :],
                     w_ffn_in[l].astype(BF16), w_ffn_out[l].astype(BF16))
    return h.reshape(batch, seq, D_MODEL)
```

```python
import functools

import jax
import jax.numpy as jnp
import numpy as np
from jax import lax
from jax.experimental import pallas as pl
from jax.experimental.pallas import tpu as pltpu

F32 = jnp.float32
BF16 = jnp.bfloat16

D_MODEL = 1024
HG_HEADS = 4
HG_DIM = 128
HG_WIDTH = HG_HEADS * HG_DIM
NSA_HEADS = 8
NSA_KV_HEADS = 2
NSA_GROUP = NSA_HEADS // NSA_KV_HEADS
NSA_DIM = 64
NSA_WIDTH = NSA_HEADS * NSA_DIM
NSA_KV_WIDTH = NSA_KV_HEADS * NSA_DIM
CMP_LEN = 32
CMP_STRIDE = 16
CMP_HIDDEN = 256
SLC_LEN = 64
SLC_TOP = 16
WINDOW = 512
N_BRANCH = 3
D_FF = 2816
ROPE_THETA = 10000.0
RMS_EPS = 1e-6
NEG = -1e30
FORCED_SCORE = 1e9
LOG2_E = 1.4426950408889634

LANES = 128
V7X_VMEM_LIMIT = 56 * 1024 * 1024

PROJ_ROWS = 512
HG_ROWS = 256
HG_CHUNK = 32
HG_SAFE_DECAY = 60.0
ATT_Q = 256
ATT_K = 256
ATT_SAFE_SCORE = 100.0
FF_TILE = 1408
K_AUG = 2 * NSA_DIM
V_AUG = NSA_DIM + 16

_HG_END = 4 * HG_WIDTH
_NQ_END = _HG_END + NSA_WIDTH
_KC_END = _NQ_END + NSA_KV_WIDTH
_VC_END = _KC_END + NSA_KV_WIDTH
_KV_END = _VC_END + 4 * NSA_KV_WIDTH
_GATE_END = _KV_END + NSA_KV_HEADS * LANES
GATES_PER_GROUP = NSA_GROUP * N_BRANCH


def _dot(a, b):
    return jnp.dot(a, b, preferred_element_type=F32)


def _dot_nt(a, b):
    return lax.dot_general(a, b, (((1,), (1,)), ((), ())), preferred_element_type=F32)


def _split3(x):
    hi = x.astype(BF16)
    r1 = x - hi.astype(F32)
    mid = r1.astype(BF16)
    lo = (r1 - mid.astype(F32)).astype(BF16)
    return hi, mid, lo


def _sigmoid(x):
    return 1.0 / (1.0 + jnp.exp(-x))


def _silu(x):
    return x * _sigmoid(x)


def _params(semantics):
    return pltpu.CompilerParams(dimension_semantics=semantics, vmem_limit_bytes=V7X_VMEM_LIMIT)


def _norm_proj_kernel(x_ref, gain_ref, w_ref, qg_ref, kg_ref, cos_ref, sin_ref, bd_ref,
                      hg_ref, kc_ref, vc_ref, gate_ref, qt_out, ks_out, vst_out, kw_out, vwt_out):
    x = x_ref[...]
    ms = jnp.mean(x * x, axis=-1, keepdims=True)
    xn = (x * lax.rsqrt(ms + RMS_EPS) * gain_ref[...]).astype(BF16)
    _nsa_prep(_dot(xn, w_ref[:, _HG_END:_NQ_END]), _dot(xn, w_ref[:, _VC_END:_KV_END]),
              qg_ref, kg_ref, cos_ref, sin_ref, bd_ref, qt_out, ks_out, vst_out, kw_out, vwt_out)
    kc_ref[...] = _dot(xn, w_ref[:, _NQ_END:_KC_END])
    vc_ref[...] = _dot(xn, w_ref[:, _KC_END:_VC_END])
    gate_ref[...] = _dot(xn, w_ref[:, _KV_END:_GATE_END])
    hg_ref[...] = _dot(xn, w_ref[:, :_HG_END])


def _norm_proj(h, gain, w_pad, q_gain, k_gain, batch, seq):
    n = h.shape[0]
    rows = min(PROJ_ROWS, seq)
    nt = seq // rows
    assert seq // SLC_LEN <= NSA_DIM, "selection one-hot must fit the spare key lanes"
    cos_t, sin_t = _rope_tables(jnp.arange(seq), NSA_HEADS)
    gid = np.arange(NSA_WIDTH) // NSA_DIM
    bd = jnp.asarray((gid[:, None] == gid[None, :]).astype(np.float32), BF16)
    qg = jnp.tile(q_gain, NSA_HEADS)[None, :]
    kg = jnp.stack([jnp.tile(k_gain[1], NSA_KV_HEADS), jnp.tile(k_gain[2], NSA_KV_HEADS)])
    widths = (_HG_END, NSA_KV_WIDTH, NSA_KV_WIDTH, NSA_KV_HEADS * LANES)
    k_shape = jax.ShapeDtypeStruct((batch, NSA_KV_HEADS, seq, K_AUG), BF16)
    k_spec = pl.BlockSpec((1, NSA_KV_HEADS, rows, K_AUG), lambda b, c: (b, 0, c, 0))
    vt_shape = jax.ShapeDtypeStruct((batch, NSA_KV_HEADS, V_AUG, seq), BF16)
    vt_spec = pl.BlockSpec((1, NSA_KV_HEADS, V_AUG, rows), lambda b, c: (b, 0, 0, c))

    def const(shape):
        return pl.BlockSpec(shape, lambda b, c: (0,) * len(shape))

    return pl.pallas_call(
        _norm_proj_kernel,
        grid=(batch, nt),
        in_specs=[
            pl.BlockSpec((rows, D_MODEL), lambda b, c: (b * nt + c, 0)),
            const((1, D_MODEL)),
            const((D_MODEL, _GATE_END)),
            const((1, NSA_WIDTH)),
            const((2, NSA_KV_WIDTH)),
            pl.BlockSpec((rows, NSA_WIDTH), lambda b, c: (c, 0)),
            pl.BlockSpec((rows, NSA_WIDTH), lambda b, c: (c, 0)),
            const((NSA_WIDTH, NSA_WIDTH)),
        ],
        out_specs=[pl.BlockSpec((rows, w), lambda b, c: (b * nt + c, 0)) for w in widths] + [
            pl.BlockSpec((1, NSA_HEADS, NSA_DIM, rows), lambda b, c: (b, 0, 0, c)),
            k_spec, vt_spec, k_spec, vt_spec,
        ],
        out_shape=[jax.ShapeDtypeStruct((n, w), F32) for w in widths] + [
            jax.ShapeDtypeStruct((batch, NSA_HEADS, NSA_DIM, seq), BF16),
            k_shape, vt_shape, k_shape, vt_shape,
        ],
        compiler_params=_params(("parallel", "parallel")),
        name="norm_proj",
    )(h, gain, w_pad, qg, kg, cos_t, sin_t, bd)


def _hgrn_kernel(q_ref, f_ref, i_ref, g_ref, lbl_ref, gn_ref, tri_ref, o_ref,
                 s_ref, b_ref, k_ref, *, layer, depth):
    rows = q_ref.shape[0]
    n_chunks = rows // HG_CHUNK
    c = HG_CHUNK

    @pl.when(pl.program_id(1) == 0)
    def _():
        s_ref[...] = jnp.zeros_like(s_ref)

    lg = lbl_ref[...]
    rows_l = [lg[i:i + 1, :] for i in range(depth)]
    mx = functools.reduce(jnp.maximum, rows_l)
    ex = [jnp.exp(r - mx) for r in rows_l]
    den = functools.reduce(lambda a, b_: a + b_, ex)
    pr = [e / den for e in ex]
    lb = functools.reduce(lambda a, b_: a + b_, pr[:layer + 1]) - pr[0]

    f = lb + (1.0 - lb) * _sigmoid(f_ref[...])
    kk = 1.0 - f
    hi, mid, lo = _split3(jnp.log(f))
    tri = tri_ref[...]
    b_all = _dot(tri, hi) + _dot(tri, mid) + _dot(tri, lo)
    needs_exact = jnp.min(b_all) < -HG_SAFE_DECAY
    gn = gn_ref[...]

    def finish(o, rows_sl, cs):
        ms = jnp.mean(o * o, axis=-1, keepdims=True)
        y = o * lax.rsqrt(ms + RMS_EPS) * gn
        o_ref[rows_sl, cs] = (y * _silu(g_ref[rows_sl, cs])).astype(o_ref.dtype)

    @pl.when(jnp.logical_not(needs_exact))
    def _():
        qf = _silu(q_ref[...])
        qt_l, kt_l, kh_l, qi_l, kb_l, dec_l = [], [], [], [], [], []
        for j in range(n_chunks // 2):
            parts = []
            for half in range(2):
                rs = slice((2 * j + half) * c, (2 * j + half + 1) * c)
                b = b_all[rs]
                tot = b[c - 1:c, :]
                parts.append((qf[rs] * jnp.exp(b), kk[rs] * jnp.exp(-b), kk[rs] * jnp.exp(tot - b), jnp.exp(tot)))
            (qa, ka, ha, ea), (qb, kb, hb, eb) = parts
            qt_l += [qa, qb]
            kt_l += [ka, kb]
            kh_l += [ha, hb]
            qi_l += [qa, qb * ea]
            kb_l += [ha * eb, hb]
            dec_l.append(ea * eb)
        qt = jnp.concatenate(qt_l, axis=0).astype(BF16)
        kt = jnp.concatenate(kt_l, axis=0).astype(BF16)
        kh = jnp.concatenate(kh_l, axis=0).astype(BF16)
        qi = jnp.concatenate(qi_l, axis=0).astype(BF16)
        kb_all = jnp.concatenate(kb_l, axis=0).astype(BF16)
        t_io = lax.broadcasted_iota(jnp.int32, (rows, rows), 0)
        s_io = lax.broadcasted_iota(jnp.int32, (rows, rows), 1)
        same_chunk = (t_io // c == s_io // c) & (s_io <= t_io)
        cross = (t_io // (2 * c) == s_io // (2 * c)) & (t_io // c > s_io // c)
        for h in range(HG_HEADS):
            cs = slice(h * HG_DIM, (h + 1) * HG_DIM)
            v = i_ref[:, cs]
            a = (jnp.where(same_chunk, _dot_nt(qt[:, cs], kt[:, cs]), 0.0)
                 + jnp.where(cross, _dot_nt(qt[:, cs], kh[:, cs]), 0.0))
            o_intra = _dot(a.astype(BF16), v.astype(BF16))
            st = s_ref[h]
            for j in range(n_chunks // 2):
                rs = slice(2 * j * c, (2 * j + 2) * c)
                o = o_intra[rs] + _dot_nt(qi[rs, cs], st.astype(BF16))
                st = st * dec_l[j][:, cs] + _dot(v[rs].T.astype(BF16), kb_all[rs, cs])
                finish(o, rs, cs)
            s_ref[h] = st

    @pl.when(needs_exact)
    def _():
        b_ref[...] = b_all
        k_ref[...] = kk
        t_io = lax.broadcasted_iota(jnp.int32, (c, c), 0)
        s_io = lax.broadcasted_iota(jnp.int32, (c, c), 1)
        row_io = lax.broadcasted_iota(jnp.int32, (c, HG_DIM), 0)

        def chunk_step(ci, _):
            r0 = pl.multiple_of(ci * c, c)
            for h in range(HG_HEADS):
                cs = slice(h * HG_DIM, (h + 1) * HG_DIM)
                qf = _silu(q_ref[pl.ds(r0, c), cs])
                b = b_ref[pl.ds(r0, c), cs]
                k = k_ref[pl.ds(r0, c), cs]
                v = i_ref[pl.ds(r0, c), cs]
                b_last = b[c - 1:c, :]
                a = jnp.zeros((c, c), F32)
                for s in range(c):
                    d = jnp.where(row_io >= s, b - b[s:s + 1, :], NEG)
                    col = jnp.sum(qf * k[s:s + 1, :] * jnp.exp(d), axis=-1, keepdims=True)
                    a = jnp.where(s_io == s, col, a)
                st = s_ref[h]
                o = _dot(a.astype(BF16), v.astype(BF16)) + _dot_nt((qf * jnp.exp(b)).astype(BF16), st.astype(BF16))
                s_ref[h] = st * jnp.exp(b_last) + _dot(v.T.astype(BF16), (k * jnp.exp(b_last - b)).astype(BF16))
                finish(o, pl.ds(r0, c), cs)

        lax.fori_loop(0, n_chunks, chunk_step, None)


def _hgrn(hgp, lb_logits, gnorm, layer, batch, seq):
    n = hgp.shape[0]
    rows = min(HG_ROWS, seq)
    nt = seq // rows
    depth = lb_logits.shape[0]
    idx = np.arange(rows)
    tri = ((idx[:, None] >= idx[None, :]) & (idx[:, None] // HG_CHUNK == idx[None, :] // HG_CHUNK))
    tri = jnp.asarray(tri.astype(np.float32), BF16)

    def col_spec(j):
        return pl.BlockSpec((rows, HG_WIDTH), lambda b, c, j=j: (b * nt + c, j))

    return pl.pallas_call(
        functools.partial(_hgrn_kernel, layer=layer, depth=depth),
        grid=(batch, nt),
        in_specs=[
            col_spec(0), col_spec(1), col_spec(2), col_spec(3),
            pl.BlockSpec((depth, HG_WIDTH), lambda b, c: (0, 0)),
            pl.BlockSpec((1, HG_DIM), lambda b, c: (0, 0)),
            pl.BlockSpec((rows, rows), lambda b, c: (0, 0)),
        ],
        out_specs=pl.BlockSpec((rows, HG_WIDTH), lambda b, c: (b * nt + c, 0)),
        out_shape=jax.ShapeDtypeStruct((n, HG_WIDTH), BF16),
        scratch_shapes=[
            pltpu.VMEM((HG_HEADS, HG_DIM, HG_DIM), F32),
            pltpu.VMEM((rows, HG_WIDTH), F32),
            pltpu.VMEM((rows, HG_WIDTH), F32),
        ],
        compiler_params=_params(("parallel", "arbitrary")),
        name="hgrn",
    )(hgp, hgp, hgp, hgp, lb_logits, gnorm, tri)


def _group_rms(x, bd, gain):
    sq = x * x
    hi = sq.astype(BF16)
    lo = (sq - hi.astype(F32)).astype(BF16)
    ss = _dot(hi, bd) + _dot(lo, bd)
    return x * lax.rsqrt(ss * (1.0 / NSA_DIM) + RMS_EPS) * gain


def _rope_lanes(x, cos, sin_signed):
    w = x.shape[-1]
    half = NSA_DIM // 2
    lane = lax.broadcasted_iota(jnp.int32, x.shape, 1)
    first = (lane & (NSA_DIM - 1)) < half
    swapped = jnp.where(first, pltpu.roll(x, w - half, 1), pltpu.roll(x, half, 1))
    return x * cos + swapped * sin_signed


def _nsa_prep(nq, kv, qg_ref, kg_ref, cos_ref, sin_ref, bd_ref, qt_out, ks_out, vst_out, kw_out, vwt_out):
    cos = cos_ref[...]
    sin = sin_ref[...]
    bd = bd_ref[...]
    kvw = NSA_KV_WIDTH
    rows = nq.shape[0]
    q = _rope_lanes(_group_rms(nq, bd, qg_ref[...]), cos, sin) * (NSA_DIM ** -0.5 * LOG2_E)
    qt_out[0] = q.T.reshape(NSA_HEADS, NSA_DIM, rows).astype(BF16)
    bdk = bd[:kvw, :kvw]
    cosk = cos[:, :kvw]
    sink = sin[:, :kvw]
    ks = _rope_lanes(_group_rms(kv[:, 0:kvw], bdk, kg_ref[0:1, :]), cosk, sink)
    kw = _rope_lanes(_group_rms(kv[:, 2 * kvw:3 * kvw], bdk, kg_ref[1:2, :]), cosk, sink)
    pad_rows = V_AUG - NSA_DIM
    ones_row = jnp.where(lax.broadcasted_iota(jnp.int32, (NSA_KV_HEADS, pad_rows, rows), 1) == 0, 1.0, 0.0)

    def values_t(v):
        vt = v.T.reshape(NSA_KV_HEADS, NSA_DIM, rows)
        return jnp.concatenate([vt, ones_row], axis=1).astype(BF16)

    vst_out[0] = values_t(kv[:, kvw:2 * kvw])
    vwt_out[0] = values_t(kv[:, 3 * kvw:4 * kvw])
    tok = pl.program_id(1) * rows + lax.broadcasted_iota(jnp.int32, (rows, NSA_DIM), 0)
    onehot = jnp.where(lax.broadcasted_iota(jnp.int32, (rows, NSA_DIM), 1) == tok // SLC_LEN, 1.0, 0.0)
    zeros = jnp.zeros((rows, NSA_DIM), F32)
    for g in range(NSA_KV_HEADS):
        sl = slice(g * NSA_DIM, (g + 1) * NSA_DIM)
        ks_out[0, g] = jnp.concatenate([ks[:, sl], onehot], axis=1).astype(BF16)
        kw_out[0, g] = jnp.concatenate([kw[:, sl], zeros], axis=1).astype(BF16)


def _rope_tables(pos, reps):
    half = NSA_DIM // 2
    freqs = ROPE_THETA ** (-jnp.arange(half, dtype=F32) / half)
    ang = pos.astype(F32)[:, None] * freqs[None, :]
    cos = jnp.cos(ang)
    sin = jnp.sin(ang)
    cos_t = jnp.tile(jnp.concatenate([cos, cos], axis=1), (1, reps))
    sin_t = jnp.tile(jnp.concatenate([-sin, sin], axis=1), (1, reps))
    return cos_t, sin_t


def _nsa_cmp_kernel(kseg_ref, vseg_ref, pek_ref, pev_ref, w1k_ref, w1v_ref, w2k_ref, w2v_ref,
                    kg_ref, cos_ref, sin_ref, kc_out, vct_out):
    nseg = kseg_ref.shape[1]

    def hidden(seg, pe_ref, w1_ref):
        y0 = _dot((seg + pe_ref[0:1, :]).astype(BF16), w1_ref[0])
        y1 = _dot((seg + pe_ref[1:2, :]).astype(BF16), w1_ref[1])
        act = _silu(y0 + pltpu.roll(y1, nseg - 1, 0)).astype(BF16)
        return [act[:, g * CMP_HIDDEN:(g + 1) * CMP_HIDDEN] for g in range(NSA_KV_HEADS)]

    hk = hidden(kseg_ref[0], pek_ref, w1k_ref)
    hv = hidden(vseg_ref[0], pev_ref, w1v_ref)
    half = NSA_DIM // 2
    for g in range(NSA_KV_HEADS):
        x = _dot(hk[g], w2k_ref[...])
        ms = jnp.mean(x * x, axis=-1, keepdims=True)
        xn = x * lax.rsqrt(ms + RMS_EPS) * kg_ref[...]
        swapped = jnp.concatenate([xn[:, half:], xn[:, :half]], axis=1)
        kc_out[0, g] = (xn * cos_ref[...] + swapped * sin_ref[...]).astype(BF16)
        vct_out[0, g] = _dot_nt(w2v_ref[...], hv[g]).astype(BF16)


def _cmp_weights(pe, w1):
    pe_r = pe.reshape(2, CMP_STRIDE, 1, NSA_DIM)
    pe_seg = jnp.broadcast_to(pe_r, (2, CMP_STRIDE, NSA_KV_HEADS, NSA_DIM)).reshape(2, CMP_STRIDE * NSA_KV_WIDTH)
    w1r = w1.reshape(2, CMP_STRIDE, NSA_DIM, CMP_HIDDEN)
    eye = jnp.eye(NSA_KV_HEADS, dtype=w1.dtype)
    wbig = jnp.einsum('rldj,gh->rlgdhj', w1r, eye)
    wbig = wbig.reshape(2, CMP_STRIDE * NSA_KV_WIDTH, NSA_KV_HEADS * CMP_HIDDEN)
    return pe_seg, wbig.astype(BF16)


def _nsa_cmp(kc_tok, vc_tok, pe_k, w1_k, w2_k, pe_v, w1_v, w2_v, k_gain0, batch, seq):
    nseg = seq // CMP_STRIDE
    segw = CMP_STRIDE * NSA_KV_WIDTH
    kseg = kc_tok.reshape(batch, nseg, segw)
    vseg = vc_tok.reshape(batch, nseg, segw)
    pek, w1k = _cmp_weights(pe_k, w1_k)
    pev, w1v = _cmp_weights(pe_v, w1_v)
    cmp_end = jnp.arange(nseg) * CMP_STRIDE + CMP_LEN - 1
    cos_t, sin_t = _rope_tables(cmp_end, 1)
    hid2 = NSA_KV_HEADS * CMP_HIDDEN
    seg_spec = pl.BlockSpec((1, nseg, segw), lambda b: (b, 0, 0))
    out_spec = pl.BlockSpec((1, NSA_KV_HEADS, nseg, NSA_DIM), lambda b: (b, 0, 0, 0))
    out_shape = jax.ShapeDtypeStruct((batch, NSA_KV_HEADS, nseg, NSA_DIM), BF16)
    out_t_spec = pl.BlockSpec((1, NSA_KV_HEADS, NSA_DIM, nseg), lambda b: (b, 0, 0, 0))
    out_t_shape = jax.ShapeDtypeStruct((batch, NSA_KV_HEADS, NSA_DIM, nseg), BF16)
    return pl.pallas_call(
        _nsa_cmp_kernel,
        grid=(batch,),
        in_specs=[
            seg_spec, seg_spec,
            pl.BlockSpec((2, segw), lambda b: (0, 0)),
            pl.BlockSpec((2, segw), lambda b: (0, 0)),
            pl.BlockSpec((2, segw, hid2), lambda b: (0, 0, 0)),
            pl.BlockSpec((2, segw, hid2), lambda b: (0, 0, 0)),
            pl.BlockSpec((CMP_HIDDEN, NSA_DIM), lambda b: (0, 0)),
            pl.BlockSpec((NSA_DIM, CMP_HIDDEN), lambda b: (0, 0)),
            pl.BlockSpec((1, NSA_DIM), lambda b: (0, 0)),
            pl.BlockSpec((nseg, NSA_DIM), lambda b: (0, 0)),
            pl.BlockSpec((nseg, NSA_DIM), lambda b: (0, 0)),
        ],
        out_specs=[out_spec, out_t_spec],
        out_shape=[out_shape, out_t_shape],
        compiler_params=_params(("parallel",)),
        name="nsa_cmp",
    )(kseg, vseg, pek, pev, w1k, w1v, w2_k.astype(BF16), w2_v.T.astype(BF16),
      k_gain0[None, :], cos_t, sin_t)


def _nsa_attn_kernel(qt_ref, kc_ref, vct_ref, ks_ref, vst_ref, kw_ref, vwt_ref, gate_ref,
                     mcs_ref, bound_ok_ref, o_ref, *, seq, top):
    tq = qt_ref.shape[3]
    tk = min(ATT_K, seq)
    nseg = kc_ref.shape[2]
    nslc = mcs_ref.shape[0]
    t0 = pl.program_id(1) * tq
    groups = range(NSA_KV_HEADS)
    qts = [jnp.concatenate([qt_ref[0, g * NSA_GROUP + h] for h in range(NSA_GROUP)], axis=1) for g in groups]

    def tile_heads(x):
        return jnp.concatenate([x] * NSA_GROUP, axis=1)

    blk_n = lax.broadcasted_iota(jnp.int32, (nseg, tq), 0)
    t_col = t0 + lax.broadcasted_iota(jnp.int32, (nseg, tq), 1)
    valid = (blk_n * CMP_STRIDE + (CMP_LEN - 1)) <= t_col
    j_io = lax.broadcasted_iota(jnp.int32, (nslc, tq), 0)
    cur = (t0 + lax.broadcasted_iota(jnp.int32, (nslc, tq), 1)) // SLC_LEN
    forced = (j_io == 0) | (j_io == cur) | (j_io == cur - 1)
    mcs = mcs_ref[...]
    o_cmps = []
    q_aug = []
    for g in groups:
        sc = _dot(kc_ref[0, g], qts[g])
        p_heads = []
        for h in range(NSA_GROUP):
            sm = jnp.where(valid, sc[:, h * tq:(h + 1) * tq], NEG)
            p = jnp.where(valid, jnp.exp2(sm - jnp.max(sm, axis=0, keepdims=True)), 0.0)
            p_heads.append(p / jnp.maximum(jnp.sum(p, axis=0, keepdims=True), 1e-30))
        o_cmps.append(_dot(vct_ref[0, g], jnp.concatenate(p_heads, axis=1).astype(BF16)))

        p_sum = functools.reduce(lambda a, b: a + b, p_heads)
        imp = functools.reduce(lambda a, b: a + b, [_dot(mcs, piece) for piece in _split3(p_sum)])
        score = jnp.where(forced, FORCED_SCORE, jnp.where(j_io <= cur, imp, NEG))
        rank = jnp.zeros((nslc, tq), F32)
        for jp in range(nslc):
            other = score[jp:jp + 1, :]
            ahead = (other > score) | ((other == score) & (j_io > jp))
            rank = rank + jnp.where(ahead, 1.0, 0.0)
        sel_bias = jnp.where(rank < top, 0.0, NEG)
        sel_rows = jnp.concatenate([sel_bias, jnp.zeros((K_AUG - NSA_DIM - nslc, tq), F32)], axis=0)
        q_aug.append(jnp.concatenate([qts[g], tile_heads(sel_rows).astype(BF16)], axis=0))

    lanes = NSA_GROUP * tq
    init1 = (jnp.full((1, lanes), NEG, F32), jnp.zeros((V_AUG, lanes), F32))
    init = tuple((init1, init1) for _ in groups)
    k_io = lax.broadcasted_iota(jnp.int32, (tk, tq), 0)
    t_io = t0 + lax.broadcasted_iota(jnp.int32, (tk, tq), 1)
    diag = t0 // tk
    n_win = -(-(WINDOW - 1) // tk) + 1

    def online_step(carry, s, vt):
        m_old, acc = carry
        m_new = jnp.maximum(m_old, jnp.max(s, axis=0, keepdims=True))
        pe = jnp.exp2(s - m_new).astype(BF16)
        return m_new, jnp.exp2(m_old - m_new) * acc + _dot(vt, pe)

    def bounded_step(carry, s, vt):
        return carry[0], carry[1] + _dot(vt, jnp.exp2(s).astype(BF16))

    def tile_step(kt, carries, window, causal=False, span=1, online_step=online_step):
        k0 = pl.multiple_of(kt * tk, tk)
        kpos = k0 + k_io
        out = []
        for g in groups:
            c_slc, c_swa = carries[g]
            if window:
                keys = jnp.concatenate([ks_ref[0, g, pl.ds(k0, tk), :], kw_ref[0, g, pl.ds(k0, tk), :]], axis=0)
                s2 = _dot(keys, q_aug[g])
                s_slc = s2[:tk]
                if causal:
                    s_slc = s_slc + tile_heads(jnp.where(kpos <= t_io, 0.0, NEG))
                wbias = jnp.where((kpos <= t_io) & (kpos > t_io - WINDOW), 0.0, NEG)
                c_slc = online_step(c_slc, s_slc, vst_ref[0, g, :, pl.ds(k0, tk)])
                c_swa = online_step(c_swa, s2[tk:] + tile_heads(wbias), vwt_ref[0, g, :, pl.ds(k0, tk)])
            else:
                s = _dot(ks_ref[0, g, pl.ds(k0, span * tk), :], q_aug[g])
                c_slc = online_step(c_slc, s, vst_ref[0, g, :, pl.ds(k0, span * tk)])
            out.append((c_slc, c_swa))
        return tuple(out)

    rest =jnp.maximum(diag - (n_win - 1), 0)

    def attend(step):
        carries = tile_step(diag, init, True, causal=True, online_step=step)
        carries = lax.fori_loop(
            1, jnp.minimum(n_win, diag + 1), lambda i, c: tile_step(diag - i, c, True, online_step=step), carries)
        carries = lax.fori_loop(
            0, rest // 2, lambda p, c: tile_step(rest - 2 - 2 * p, c, False, span=2, online_step=step), carries)
        carries = lax.cond(rest % 2 == 1, lambda c: tile_step(0, c, False, online_step=step), lambda c: c, carries)

        grp_w = NSA_GROUP * NSA_DIM
        for g in groups:
            (_, acc_s), (_, acc_w) = carries[g]
            o_slc = acc_s[:NSA_DIM] / acc_s[NSA_DIM:NSA_DIM + 1]
            o_swa = acc_w[:NSA_DIM] / acc_w[NSA_DIM:NSA_DIM + 1]
            gate = _sigmoid(gate_ref[:, g * LANES:(g + 1) * LANES].T)
            outs = []
            for h in range(NSA_GROUP):
                ls = slice(h * tq, (h + 1) * tq)
                g0 = gate[N_BRANCH * h:N_BRANCH * h + 1, :]
                g1 = gate[N_BRANCH * h + 1:N_BRANCH * h + 2, :]
                g2 = gate[N_BRANCH * h + 2:N_BRANCH * h + 3, :]
                outs.append(g0 * o_cmps[g][:, ls] + g1 * o_slc[:, ls] + g2 * o_swa[:, ls])
            o_ref[:, g * grp_w:(g + 1) * grp_w] = jnp.concatenate(outs, axis=0).T.astype(o_ref.dtype)

    bounded = bound_ok_ref[0] != 0

    @pl.when(bounded)
    def _():
        attend(bounded_step)

    @pl.when(jnp.logical_not(bounded))
    def _():
        attend(online_step)


def _score_bound_ok(q_gain, k_gain):
    k_max = jnp.max(jnp.abs(k_gain[1:]))
    bound = NSA_DIM * jnp.max(jnp.abs(q_gain)) * k_max * (NSA_DIM ** -0.5 * LOG2_E)
    return (bound <= ATT_SAFE_SCORE).astype(jnp.int32).reshape(1)


def _nsa_attn(qt, kc, vct, ks, vst, kw, vwt, gate, bound_ok, batch, seq):
    tq = min(ATT_Q, seq)
    nq = seq // tq
    nseg = seq // CMP_STRIDE
    nslc = seq // SLC_LEN
    top = min(SLC_TOP, nslc)
    c_start = np.arange(nseg) * CMP_STRIDE
    c_end = c_start + CMP_LEN - 1
    s_start = np.arange(nslc) * SLC_LEN
    s_end = s_start + SLC_LEN - 1
    overlap = (c_start[None, :] <= s_end[:, None]) & (c_end[None, :] >= s_start[:, None])
    overlap[:, nseg - 1] = False
    mcs = jnp.asarray(overlap.astype(np.float32), BF16)
    k_full = pl.BlockSpec((1, NSA_KV_HEADS, seq, K_AUG), lambda b, i: (b, 0, 0, 0))
    vt_full = pl.BlockSpec((1, NSA_KV_HEADS, V_AUG, seq), lambda b, i: (b, 0, 0, 0))
    return pl.pallas_call(
        functools.partial(_nsa_attn_kernel, seq=seq, top=top),
        grid=(batch, nq),
        in_specs=[
            pl.BlockSpec((1, NSA_HEADS, NSA_DIM, tq), lambda b, i: (b, 0, 0, i)),
            pl.BlockSpec((1, NSA_KV_HEADS, nseg, NSA_DIM), lambda b, i: (b, 0, 0, 0)),
            pl.BlockSpec((1, NSA_KV_HEADS, NSA_DIM, nseg), lambda b, i: (b, 0, 0, 0)),
            k_full, vt_full, k_full, vt_full,
            pl.BlockSpec((tq, NSA_KV_HEADS * LANES), lambda b, i: (b * nq + i, 0)),
            pl.BlockSpec((nslc, nseg), lambda b, i: (0, 0)),
            pl.BlockSpec(memory_space=pltpu.SMEM),
        ],
        out_specs=pl.BlockSpec((tq, NSA_WIDTH), lambda b, i: (b * nq + i, 0)),
        out_shape=jax.ShapeDtypeStruct((batch * seq, NSA_WIDTH), BF16),
        compiler_params=_params(("parallel", "arbitrary")),
        name="nsa_attn",
    )(qt, kc, vct, ks, vst, kw, vwt, gate, mcs, bound_ok)


def _out_ffn_kernel(h_ref, ohg_ref, onsa_ref, wout_ref, gain_ref, wg_ref, wu_ref, wo_ref,
                    o_ref, xn_ref, acc_ref):
    j = pl.program_id(1)

    @pl.when(j == 0)
    def _():
        h1 = (h_ref[...] + _dot(ohg_ref[...], wout_ref[:HG_WIDTH, :])
              + _dot(onsa_ref[...], wout_ref[HG_WIDTH:, :]))
        acc_ref[...] = h1
        ms = jnp.mean(h1 * h1, axis=-1, keepdims=True)
        xn_ref[...] = (h1 * lax.rsqrt(ms + RMS_EPS) * gain_ref[...]).astype(BF16)

    xn = xn_ref[...]
    act = (_silu(_dot(xn, wg_ref[...])) * _dot(xn, wu_ref[...])).astype(BF16)
    acc_ref[...] += _dot(act, wo_ref[...])

    @pl.when(j == pl.num_programs(1) - 1)
    def _():
        o_ref[...] = acc_ref[...]


def _out_ffn(h, o_hg, o_nsa, w_out, gain, w_ffn_in, w_ffn_out):
    n = h.shape[0]
    tm = PROJ_ROWS
    nff = D_FF // FF_TILE
    mixw = HG_WIDTH + NSA_WIDTH
    return pl.pallas_call(
        _out_ffn_kernel,
        grid=(n // tm, nff),
        in_specs=[
            pl.BlockSpec((tm, D_MODEL), lambda i, j: (i, 0)),
            pl.BlockSpec((tm, HG_WIDTH), lambda i, j: (i, 0)),
            pl.BlockSpec((tm, NSA_WIDTH), lambda i, j: (i, 0)),
            pl.BlockSpec((mixw, D_MODEL), lambda i, j: (0, 0)),
            pl.BlockSpec((1, D_MODEL), lambda i, j: (0, 0)),
            pl.BlockSpec((D_MODEL, FF_TILE), lambda i, j: (0, j)),
            pl.BlockSpec((D_MODEL, FF_TILE), lambda i, j: (0, nff + j)),
            pl.BlockSpec((FF_TILE, D_MODEL), lambda i, j: (j, 0)),
        ],
        out_specs=pl.BlockSpec((tm, D_MODEL), lambda i, j: (i, 0)),
        out_shape=jax.ShapeDtypeStruct((n, D_MODEL), F32),
        scratch_shapes=[pltpu.VMEM((tm, D_MODEL), BF16), pltpu.VMEM((tm, D_MODEL), F32)],
        compiler_params=_params(("parallel", "arbitrary")),
        name="out_ffn",
    )(h, o_hg, o_nsa, w_out, gain, w_ffn_in, w_ffn_in, w_ffn_out)


def _pad_w_in(w):
    pieces = [w[:, :_KV_END]]
    for g in range(NSA_KV_HEADS):
        cols = w[:, _KV_END + g * GATES_PER_GROUP:_KV_END + (g + 1) * GATES_PER_GROUP]
        pieces.append(jnp.pad(cols, ((0, 0), (0, LANES - GATES_PER_GROUP))))
    return jnp.concatenate(pieces, axis=1).astype(BF16)


def kernel(x, w_in, w_out, hg_lb_logits, hg_gnorm, q_gain, k_gain, cmp_pe_k, cmp_w1_k, cmp_w2_k,
           cmp_pe_v, cmp_w1_v, cmp_w2_v, w_ffn_in, w_ffn_out, norm_mix, norm_ffn):
    batch, seq, _ = x.shape
    depth = w_in.shape[0]
    h = x.reshape(batch * seq, D_MODEL)
    for l in range(depth):
        hgp, kc_tok, vc_tok, gate, qt, ks, vst, kw, vwt = _norm_proj(
            h, norm_mix[l][None, :], _pad_w_in(w_in[l]), q_gain[l], k_gain[l], batch, seq)
        o_hg = _hgrn(hgp, hg_lb_logits, hg_gnorm[l][None, :], l, batch, seq)
        kc, vct = _nsa_cmp(kc_tok, vc_tok, cmp_pe_k[l], cmp_w1_k[l], cmp_w2_k[l],
                           cmp_pe_v[l], cmp_w1_v[l], cmp_w2_v[l], k_gain[l, 0], batch, seq)
        o_nsa = _nsa_attn(qt, kc, vct, ks, vst, kw, vwt, gate, _score_bound_ok(q_gain[l], k_gain[l]), batch, seq)
        h = _out_ffn(h, o_hg, o_nsa, w_out[l].astype(BF16), norm_ffn[l][None, :],
                     w_ffn_in[l].astype(BF16), w_ffn_out[l].astype(BF16))
    return h.reshape(batch, seq, D_MODEL)
```

```python
import functools

import jax
import jax.numpy as jnp
import numpy as np
from jax import lax
from jax.experimental import pallas as pl
from jax.experimental.pallas import tpu as pltpu

F32 = jnp.float32
BF16 = jnp.bfloat16

D_MODEL = 1024
HG_HEADS = 4
HG_DIM = 128
HG_WIDTH = HG_HEADS * HG_DIM
NSA_HEADS = 8
NSA_KV_HEADS = 2
NSA_GROUP = NSA_HEADS // NSA_KV_HEADS
NSA_DIM = 64
NSA_WIDTH = NSA_HEADS * NSA_DIM
NSA_KV_WIDTH = NSA_KV_HEADS * NSA_DIM
CMP_LEN = 32
CMP_STRIDE = 16
CMP_HIDDEN = 256
SLC_LEN = 64
SLC_TOP = 16
WINDOW = 512
N_BRANCH = 3
D_FF = 2816
ROPE_THETA = 10000.0
RMS_EPS = 1e-6
NEG = -1e30
FORCED_SCORE = 1e9
LOG2_E = 1.4426950408889634

LANES = 128
V7X_VMEM_LIMIT = 56 * 1024 * 1024

PROJ_ROWS = 512
HG_ROWS = 256
HG_CHUNK = 32
HG_SAFE_DECAY = 60.0
ATT_Q = 256
ATT_K = 256
ATT_SAFE_SCORE = 100.0
FF_TILE = 1408
K_AUG = 2 * NSA_DIM
V_AUG = NSA_DIM + 16

_HG_END = 4 * HG_WIDTH
_NQ_END = _HG_END + NSA_WIDTH
_KC_END = _NQ_END + NSA_KV_WIDTH
_VC_END = _KC_END + NSA_KV_WIDTH
_KV_END = _VC_END + 4 * NSA_KV_WIDTH
_GATE_END = _KV_END + NSA_KV_HEADS * LANES
GATES_PER_GROUP = NSA_GROUP * N_BRANCH


def _dot(a, b):
    return jnp.dot(a, b, preferred_element_type=F32)


def _dot_nt(a, b):
    return lax.dot_general(a, b, (((1,), (1,)), ((), ())), preferred_element_type=F32)


def _split3(x):
    hi = x.astype(BF16)
    r1 = x - hi.astype(F32)
    mid = r1.astype(BF16)
    lo = (r1 - mid.astype(F32)).astype(BF16)
    return hi, mid, lo


def _sigmoid(x):
    return 1.0 / (1.0 + jnp.exp(-x))


def _silu(x):
    return x * _sigmoid(x)


def _params(semantics):
    return pltpu.CompilerParams(dimension_semantics=semantics, vmem_limit_bytes=V7X_VMEM_LIMIT)


def _norm_proj_kernel(x_ref, gain_ref, w_ref, qg_ref, kg_ref, cos_ref, sin_ref, bd_ref,
                      hg_ref, kseg_ref, vseg_ref, gate_ref, qt_out, ks_out, vst_out, kw_out, vwt_out,
                      kc_scr, vc_scr):
    x = x_ref[...]
    ms = jnp.mean(x * x, axis=-1, keepdims=True)
    xn = (x * lax.rsqrt(ms + RMS_EPS) * gain_ref[...]).astype(BF16)
    _nsa_prep(_dot(xn, w_ref[:, _HG_END:_NQ_END]), _dot(xn, w_ref[:, _VC_END:_KV_END]),
              qg_ref, kg_ref, cos_ref, sin_ref, bd_ref, qt_out, ks_out, vst_out, kw_out, vwt_out)
    kc_scr[...] = _dot(xn, w_ref[:, _NQ_END:_KC_END])
    vc_scr[...] = _dot(xn, w_ref[:, _KC_END:_VC_END])
    nseg = kseg_ref.shape[0]
    for tok in range(CMP_STRIDE):
        cols = slice(tok * NSA_KV_WIDTH, (tok + 1) * NSA_KV_WIDTH)
        kseg_ref[:, cols] = kc_scr[pl.ds(tok, nseg, stride=CMP_STRIDE), :]
        vseg_ref[:, cols] = vc_scr[pl.ds(tok, nseg, stride=CMP_STRIDE), :]
    gate_ref[...] = _dot(xn, w_ref[:, _KV_END:_GATE_END])
    hg_ref[...] = _dot(xn, w_ref[:, :_HG_END])


def _norm_proj(h, gain, w_pad, q_gain, k_gain, batch, seq):
    n = h.shape[0]
    rows = min(PROJ_ROWS, seq)
    nt = seq // rows
    assert seq // SLC_LEN <= NSA_DIM, "selection one-hot must fit the spare key lanes"
    cos_t, sin_t = _rope_tables(jnp.arange(seq), NSA_HEADS)
    gid = np.arange(NSA_WIDTH) // NSA_DIM
    bd = jnp.asarray((gid[:, None] == gid[None, :]).astype(np.float32), BF16)
    qg = jnp.tile(q_gain, NSA_HEADS)[None, :]
    kg = jnp.stack([jnp.tile(k_gain[1], NSA_KV_HEADS), jnp.tile(k_gain[2], NSA_KV_HEADS)])
    segw = CMP_STRIDE * NSA_KV_WIDTH
    f32_outs = ((rows, n, _HG_END), (rows // CMP_STRIDE, n // CMP_STRIDE, segw),
                (rows // CMP_STRIDE, n // CMP_STRIDE, segw), (rows, n, NSA_KV_HEADS * LANES))
    k_shape =jax.ShapeDtypeStruct((batch, NSA_KV_HEADS, seq, K_AUG), BF16)
    k_spec = pl.BlockSpec((1, NSA_KV_HEADS, rows, K_AUG), lambda b, c: (b, 0, c, 0))
    vt_shape = jax.ShapeDtypeStruct((batch, NSA_KV_HEADS, V_AUG, seq), BF16)
    vt_spec = pl.BlockSpec((1, NSA_KV_HEADS, V_AUG, rows), lambda b, c: (b, 0, 0, c))

    def const(shape):
        return pl.BlockSpec(shape, lambda b, c: (0,) * len(shape))

    return pl.pallas_call(
        _norm_proj_kernel,
        grid=(batch, nt),
        in_specs=[
            pl.BlockSpec((rows, D_MODEL), lambda b, c: (b * nt + c, 0)),
            const((1, D_MODEL)),
            const((D_MODEL, _GATE_END)),
            const((1, NSA_WIDTH)),
            const((2, NSA_KV_WIDTH)),
            pl.BlockSpec((rows, NSA_WIDTH), lambda b, c: (c, 0)),
            pl.BlockSpec((rows, NSA_WIDTH), lambda b, c: (c, 0)),
            const((NSA_WIDTH, NSA_WIDTH)),
        ],
        out_specs=[pl.BlockSpec((r, w), lambda b, c: (b * nt + c, 0)) for r, _, w in f32_outs] + [
            pl.BlockSpec((1, NSA_HEADS, NSA_DIM, rows), lambda b, c: (b, 0, 0, c)),
            k_spec, vt_spec, k_spec, vt_spec,
        ],
        out_shape=[jax.ShapeDtypeStruct((total, w), F32) for _, total, w in f32_outs] + [
            jax.ShapeDtypeStruct((batch, NSA_HEADS, NSA_DIM, seq), BF16),
            k_shape, vt_shape, k_shape, vt_shape,
        ],
        scratch_shapes=[pltpu.VMEM((rows, NSA_KV_WIDTH), F32), pltpu.VMEM((rows, NSA_KV_WIDTH), F32)],
        compiler_params=_params(("parallel", "parallel")),
        name="norm_proj",
    )(h, gain, w_pad, qg, kg, cos_t, sin_t, bd)


def _hgrn_kernel(q_ref, f_ref, i_ref, g_ref, lbl_ref, gn_ref, tri_ref, o_ref,
                 s_ref, b_ref, k_ref, *, layer, depth):
    rows = q_ref.shape[0]
    n_chunks = rows // HG_CHUNK
    c = HG_CHUNK

    @pl.when(pl.program_id(1) == 0)
    def _():
        s_ref[...] = jnp.zeros_like(s_ref)

    lg = lbl_ref[...]
    rows_l = [lg[i:i + 1, :] for i in range(depth)]
    mx = functools.reduce(jnp.maximum, rows_l)
    ex = [jnp.exp(r - mx) for r in rows_l]
    den = functools.reduce(lambda a, b_: a + b_, ex)
    pr = [e / den for e in ex]
    lb = functools.reduce(lambda a, b_: a + b_, pr[:layer + 1]) - pr[0]

    f = lb + (1.0 - lb) * _sigmoid(f_ref[...])
    kk = 1.0 - f
    w = f.shape[1]
    b3 = _dot(tri_ref[...], jnp.concatenate(_split3(jnp.log(f)), axis=1))
    b_all = b3[:, :w] + b3[:, w:2 * w] + b3[:, 2 * w:]
    needs_exact = jnp.min(b_all) < -HG_SAFE_DECAY
    gn = gn_ref[...]

    def finish(o, rows_sl, cs):
        ms = jnp.mean(o * o, axis=-1, keepdims=True)
        y = o * lax.rsqrt(ms + RMS_EPS) * gn
        o_ref[rows_sl, cs] = (y * _silu(g_ref[rows_sl, cs])).astype(o_ref.dtype)

    @pl.when(jnp.logical_not(needs_exact))
    def _():
        nb = n_chunks // 2
        blk = 2 * c
        qf = _silu(q_ref[...])
        qt_l, kt_l, kh_l, qi_l, kb_l, dec = [], [], [], [], [], []
        for j in range(nb):
            parts = []
            for half in range(2):
                rs = slice((2 * j + half) * c, (2 * j + half + 1) * c)
                b = b_all[rs]
                tot = b[c - 1:c, :]
                parts.append((qf[rs] * jnp.exp(b), kk[rs] * jnp.exp(-b), kk[rs] * jnp.exp(tot - b), jnp.exp(tot)))
            (qa, ka, ha, ea), (qb, kb, hb, eb) = parts
            qt_l += [qa, qb]
            kt_l += [ka, kb]
            kh_l += [ha, hb]
            qi_l.append(jnp.concatenate([qa, qb * ea], axis=0))
            kb_l.append(jnp.concatenate([ha * eb, hb], axis=0))
            dec.append(ea * eb)

        def span_decay(lo, hi):
            return functools.reduce(lambda x, y: x * y, dec[lo:hi]) if hi > lo else None

        def scaled(x, d):
            return x if d is None else x * d

        qt = jnp.concatenate(qt_l, axis=0).astype(BF16)
        k_in = jnp.concatenate(kt_l + kh_l, axis=0).astype(BF16)
        kb_all = jnp.concatenate(kb_l, axis=0).astype(BF16)
        q_back = jnp.concatenate(
            [scaled(qi_l[j], span_decay(max(j - dist + 1, 0), j)) for dist in range(1, nb) for j in range(nb)],
            axis=0).astype(BF16)
        q_state = jnp.concatenate([scaled(qi_l[j], span_decay(0, j)) for j in range(nb)], axis=0).astype(BF16)
        k_state = jnp.concatenate([scaled(kb_l[j], span_decay(j + 1, nb)) for j in range(nb)], axis=0).astype(BF16)
        dec_all = span_decay(0, nb)
        t_io = lax.broadcasted_iota(jnp.int32, (rows, rows), 0)
        s_io = lax.broadcasted_iota(jnp.int32, (rows, rows), 1)
        same_chunk = (t_io // c == s_io // c) & (s_io <= t_io)
        cross = (t_io // blk == s_io // blk) & (t_io // c > s_io // c)
        back = [t_io // blk - s_io // blk == dist for dist in range(1, nb)]
        for h in range(HG_HEADS):
            cs = slice(h * HG_DIM, (h + 1) * HG_DIM)
            v = i_ref[:, cs]
            near = _dot_nt(qt[:, cs], k_in[:, cs])
            a = jnp.where(same_chunk, near[:, :rows], 0.0) + jnp.where(cross, near[:, rows:], 0.0)
            if nb > 1:
                far = _dot_nt(q_back[:, cs], kb_all[:, cs])
                for i, m in enumerate(back):
                    a = a + jnp.where(m, far[i * rows:(i + 1) * rows], 0.0)
            st = s_ref[h]
            o = _dot(a.astype(BF16), v.astype(BF16)) + _dot_nt(q_state[:, cs], st.astype(BF16))
            s_ref[h] = st * dec_all[:, cs] + _dot(v.T.astype(BF16), k_state[:, cs])
            finish(o, slice(None), cs)

    @pl.when(needs_exact)
    def _():
        b_ref[...] = b_all
        k_ref[...] = kk
        t_io = lax.broadcasted_iota(jnp.int32, (c, c), 0)
        s_io = lax.broadcasted_iota(jnp.int32, (c, c), 1)
        row_io = lax.broadcasted_iota(jnp.int32, (c, HG_DIM), 0)

        def chunk_step(ci, _):
            r0 = pl.multiple_of(ci * c, c)
            for h in range(HG_HEADS):
                cs = slice(h * HG_DIM, (h + 1) * HG_DIM)
                qf = _silu(q_ref[pl.ds(r0, c), cs])
                b = b_ref[pl.ds(r0, c), cs]
                k = k_ref[pl.ds(r0, c), cs]
                v = i_ref[pl.ds(r0, c), cs]
                b_last = b[c - 1:c, :]
                a = jnp.zeros((c, c), F32)
                for s in range(c):
                    d = jnp.where(row_io >= s, b - b[s:s + 1, :], NEG)
                    col = jnp.sum(qf * k[s:s + 1, :] * jnp.exp(d), axis=-1, keepdims=True)
                    a = jnp.where(s_io == s, col, a)
                st = s_ref[h]
                o = _dot(a.astype(BF16), v.astype(BF16)) + _dot_nt((qf * jnp.exp(b)).astype(BF16), st.astype(BF16))
                s_ref[h] = st * jnp.exp(b_last) + _dot(v.T.astype(BF16), (k * jnp.exp(b_last - b)).astype(BF16))
                finish(o, pl.ds(r0, c), cs)

        lax.fori_loop(0, n_chunks, chunk_step, None)


def _hgrn(hgp, lb_logits, gnorm, layer, batch, seq):
    n = hgp.shape[0]
    rows = min(HG_ROWS, seq)
    nt = seq // rows
    depth = lb_logits.shape[0]
    idx = np.arange(rows)
    tri = ((idx[:, None] >= idx[None, :]) & (idx[:, None] // HG_CHUNK == idx[None, :] // HG_CHUNK))
    tri = jnp.asarray(tri.astype(np.float32), BF16)

    def col_spec(j):
        return pl.BlockSpec((rows, HG_WIDTH), lambda b, c, j=j: (b * nt + c, j))

    return pl.pallas_call(
        functools.partial(_hgrn_kernel, layer=layer, depth=depth),
        grid=(batch, nt),
        in_specs=[
            col_spec(0), col_spec(1), col_spec(2), col_spec(3),
            pl.BlockSpec((depth, HG_WIDTH), lambda b, c: (0, 0)),
            pl.BlockSpec((1, HG_DIM), lambda b, c: (0, 0)),
            pl.BlockSpec((rows, rows), lambda b, c: (0, 0)),
        ],
        out_specs=pl.BlockSpec((rows, HG_WIDTH), lambda b, c: (b * nt + c, 0)),
        out_shape=jax.ShapeDtypeStruct((n, HG_WIDTH), BF16),
        scratch_shapes=[
            pltpu.VMEM((HG_HEADS, HG_DIM, HG_DIM), F32),
            pltpu.VMEM((rows, HG_WIDTH), F32),
            pltpu.VMEM((rows, HG_WIDTH), F32),
        ],
        compiler_params=_params(("parallel", "arbitrary")),
        name="hgrn",
    )(hgp, hgp, hgp, hgp, lb_logits, gnorm, tri)


def _group_rms(x, bd, gain):
    sq = x * x
    hi = sq.astype(BF16)
    lo = (sq - hi.astype(F32)).astype(BF16)
    ss = _dot(hi, bd) + _dot(lo, bd)
    return x * lax.rsqrt(ss * (1.0 / NSA_DIM) + RMS_EPS) * gain


def _rope_lanes(x, cos, sin_signed):
    w = x.shape[-1]
    half = NSA_DIM // 2
    lane = lax.broadcasted_iota(jnp.int32, x.shape, 1)
    first = (lane & (NSA_DIM - 1)) < half
    swapped = jnp.where(first, pltpu.roll(x, w - half, 1), pltpu.roll(x, half, 1))
    return x * cos + swapped * sin_signed


def _nsa_prep(nq, kv, qg_ref, kg_ref, cos_ref, sin_ref, bd_ref, qt_out, ks_out, vst_out, kw_out, vwt_out):
    cos = cos_ref[...]
    sin = sin_ref[...]
    bd = bd_ref[...]
    kvw = NSA_KV_WIDTH
    rows = nq.shape[0]
    q = _rope_lanes(_group_rms(nq, bd, qg_ref[...]), cos, sin) * (NSA_DIM ** -0.5 * LOG2_E)
    qt_out[0] = q.T.reshape(NSA_HEADS, NSA_DIM, rows).astype(BF16)
    bdk = bd[:kvw, :kvw]
    cosk = cos[:, :kvw]
    sink = sin[:, :kvw]
    ks = _rope_lanes(_group_rms(kv[:, 0:kvw], bdk, kg_ref[0:1, :]), cosk, sink)
    kw = _rope_lanes(_group_rms(kv[:, 2 * kvw:3 * kvw], bdk, kg_ref[1:2, :]), cosk, sink)
    pad_rows = V_AUG - NSA_DIM
    ones_row = jnp.where(lax.broadcasted_iota(jnp.int32, (NSA_KV_HEADS, pad_rows, rows), 1) == 0, 1.0, 0.0)

    def values_t(v):
        vt = v.T.reshape(NSA_KV_HEADS, NSA_DIM, rows)
        return jnp.concatenate([vt, ones_row], axis=1).astype(BF16)

    vst_out[0] = values_t(kv[:, kvw:2 * kvw])
    vwt_out[0] = values_t(kv[:, 3 * kvw:4 * kvw])
    tok = pl.program_id(1) * rows + lax.broadcasted_iota(jnp.int32, (rows, NSA_DIM), 0)
    onehot = jnp.where(lax.broadcasted_iota(jnp.int32, (rows, NSA_DIM), 1) == tok // SLC_LEN, 1.0, 0.0)
    zeros = jnp.zeros((rows, NSA_DIM), F32)
    for g in range(NSA_KV_HEADS):
        sl = slice(g * NSA_DIM, (g + 1) * NSA_DIM)
        ks_out[0, g] = jnp.concatenate([ks[:, sl], onehot], axis=1).astype(BF16)
        kw_out[0, g] = jnp.concatenate([kw[:, sl], zeros], axis=1).astype(BF16)


def _rope_tables(pos, reps):
    half = NSA_DIM // 2
    freqs = ROPE_THETA ** (-jnp.arange(half, dtype=F32) / half)
    ang = pos.astype(F32)[:, None] * freqs[None, :]
    cos = jnp.cos(ang)
    sin = jnp.sin(ang)
    cos_t = jnp.tile(jnp.concatenate([cos, cos], axis=1), (1, reps))
    sin_t = jnp.tile(jnp.concatenate([-sin, sin], axis=1), (1, reps))
    return cos_t, sin_t


def _nsa_cmp_kernel(kseg_ref, vseg_ref, pek_ref, pev_ref, w1k_ref, w1v_ref, w2k_ref, w2v_ref,
                    kg_ref, cos_ref, sin_ref, kc_out, vct_out):
    nseg = kseg_ref.shape[1]

    def hidden(seg, pe_ref, w1_ref):
        y0 = _dot((seg + pe_ref[0:1, :]).astype(BF16), w1_ref[0])
        y1 = _dot((seg + pe_ref[1:2, :]).astype(BF16), w1_ref[1])
        act = _silu(y0 + pltpu.roll(y1, nseg - 1, 0)).astype(BF16)
        return [act[:, g * CMP_HIDDEN:(g + 1) * CMP_HIDDEN] for g in range(NSA_KV_HEADS)]

    hk = hidden(kseg_ref[0], pek_ref, w1k_ref)
    hv = hidden(vseg_ref[0], pev_ref, w1v_ref)
    half = NSA_DIM // 2
    for g in range(NSA_KV_HEADS):
        x = _dot(hk[g], w2k_ref[...])
        ms = jnp.mean(x * x, axis=-1, keepdims=True)
        xn = x * lax.rsqrt(ms + RMS_EPS) * kg_ref[...]
        swapped = jnp.concatenate([xn[:, half:], xn[:, :half]], axis=1)
        kc_out[0, g] = (xn * cos_ref[...] + swapped * sin_ref[...]).astype(BF16)
        vct_out[0, g] = _dot_nt(w2v_ref[...], hv[g]).astype(BF16)


def _cmp_weights(pe, w1):
    pe_r = pe.reshape(2, CMP_STRIDE, 1, NSA_DIM)
    pe_seg = jnp.broadcast_to(pe_r, (2, CMP_STRIDE, NSA_KV_HEADS, NSA_DIM)).reshape(2, CMP_STRIDE * NSA_KV_WIDTH)
    w1r = w1.reshape(2, CMP_STRIDE, NSA_DIM, CMP_HIDDEN)
    eye = jnp.eye(NSA_KV_HEADS, dtype=w1.dtype)
    wbig = jnp.einsum('rldj,gh->rlgdhj', w1r, eye)
    wbig = wbig.reshape(2, CMP_STRIDE * NSA_KV_WIDTH, NSA_KV_HEADS * CMP_HIDDEN)
    return pe_seg, wbig.astype(BF16)


def _nsa_cmp(kc_seg, vc_seg, pe_k, w1_k, w2_k, pe_v, w1_v, w2_v, k_gain0, batch, seq):
    nseg = seq // CMP_STRIDE
    segw = CMP_STRIDE * NSA_KV_WIDTH
    kseg = kc_seg.reshape(batch, nseg, segw)
    vseg = vc_seg.reshape(batch, nseg, segw)
    pek, w1k = _cmp_weights(pe_k, w1_k)
    pev, w1v = _cmp_weights(pe_v, w1_v)
    cmp_end = jnp.arange(nseg) * CMP_STRIDE + CMP_LEN - 1
    cos_t, sin_t = _rope_tables(cmp_end, 1)
    hid2 = NSA_KV_HEADS * CMP_HIDDEN
    seg_spec = pl.BlockSpec((1, nseg, segw), lambda b: (b, 0, 0))
    out_spec = pl.BlockSpec((1, NSA_KV_HEADS, nseg, NSA_DIM), lambda b: (b, 0, 0, 0))
    out_shape = jax.ShapeDtypeStruct((batch, NSA_KV_HEADS, nseg, NSA_DIM), BF16)
    out_t_spec = pl.BlockSpec((1, NSA_KV_HEADS, NSA_DIM, nseg), lambda b: (b, 0, 0, 0))
    out_t_shape = jax.ShapeDtypeStruct((batch, NSA_KV_HEADS, NSA_DIM, nseg), BF16)
    return pl.pallas_call(
        _nsa_cmp_kernel,
        grid=(batch,),
        in_specs=[
            seg_spec, seg_spec,
            pl.BlockSpec((2, segw), lambda b: (0, 0)),
            pl.BlockSpec((2, segw), lambda b: (0, 0)),
            pl.BlockSpec((2, segw, hid2), lambda b: (0, 0, 0)),
            pl.BlockSpec((2, segw, hid2), lambda b: (0, 0, 0)),
            pl.BlockSpec((CMP_HIDDEN, NSA_DIM), lambda b: (0, 0)),
            pl.BlockSpec((NSA_DIM, CMP_HIDDEN), lambda b: (0, 0)),
            pl.BlockSpec((1, NSA_DIM), lambda b: (0, 0)),
            pl.BlockSpec((nseg, NSA_DIM), lambda b: (0, 0)),
            pl.BlockSpec((nseg, NSA_DIM), lambda b: (0, 0)),
        ],
        out_specs=[out_spec, out_t_spec],
        out_shape=[out_shape, out_t_shape],
        compiler_params=_params(("parallel",)),
        name="nsa_cmp",
    )(kseg, vseg, pek, pev, w1k, w1v, w2_k.astype(BF16), w2_v.T.astype(BF16),
      k_gain0[None, :], cos_t, sin_t)


def _nsa_attn_kernel(qt_ref, kc_ref, vct_ref, ks_ref, vst_ref, kw_ref, vwt_ref, gate_ref,
                     mcs_ref, bound_ok_ref, o_ref, *, seq, top):
    tq = qt_ref.shape[3]
    tk = min(ATT_K, seq)
    nseg = kc_ref.shape[2]
    nslc = mcs_ref.shape[0]
    t0 = pl.program_id(1) * tq
    groups = range(NSA_KV_HEADS)
    qts = [jnp.concatenate([qt_ref[0, g * NSA_GROUP + h] for h in range(NSA_GROUP)], axis=1) for g in groups]

    def tile_heads(x):
        return jnp.concatenate([x] * NSA_GROUP, axis=1)

    blk_n = lax.broadcasted_iota(jnp.int32, (nseg, tq), 0)
    t_col = t0 + lax.broadcasted_iota(jnp.int32, (nseg, tq), 1)
    valid = (blk_n * CMP_STRIDE + (CMP_LEN - 1)) <= t_col
    j_io = lax.broadcasted_iota(jnp.int32, (nslc, tq), 0)
    cur = (t0 + lax.broadcasted_iota(jnp.int32, (nslc, tq), 1)) // SLC_LEN
    forced = (j_io == 0) | (j_io == cur) | (j_io == cur - 1)
    mcs = mcs_ref[...]
    o_cmps = []
    q_aug = []
    for g in groups:
        sc = _dot(kc_ref[0, g], qts[g])
        p_heads = []
        for h in range(NSA_GROUP):
            sm = jnp.where(valid, sc[:, h * tq:(h + 1) * tq], NEG)
            p = jnp.where(valid, jnp.exp2(sm - jnp.max(sm, axis=0, keepdims=True)), 0.0)
            p_heads.append(p / jnp.maximum(jnp.sum(p, axis=0, keepdims=True), 1e-30))
        o_cmps.append(_dot(vct_ref[0, g], jnp.concatenate(p_heads, axis=1).astype(BF16)))

        p_sum = functools.reduce(lambda a, b: a + b, p_heads)
        imp = functools.reduce(lambda a, b: a + b, [_dot(mcs, piece) for piece in _split3(p_sum)])
        score = jnp.where(forced, FORCED_SCORE, jnp.where(j_io <= cur, imp, NEG))
        rank = jnp.zeros((nslc, tq), F32)
        for jp in range(nslc):
            other = score[jp:jp + 1, :]
            ahead = (other > score) | ((other == score) & (j_io > jp))
            rank = rank + jnp.where(ahead, 1.0, 0.0)
        sel_bias = jnp.where(rank < top, 0.0, NEG)
        sel_rows = jnp.concatenate([sel_bias, jnp.zeros((K_AUG - NSA_DIM - nslc, tq), F32)], axis=0)
        q_aug.append(jnp.concatenate([qts[g], tile_heads(sel_rows).astype(BF16)], axis=0))

    lanes = NSA_GROUP * tq
    init1 = (jnp.full((1, lanes), NEG, F32), jnp.zeros((V_AUG, lanes), F32))
    init = tuple((init1, init1) for _ in groups)
    k_io = lax.broadcasted_iota(jnp.int32, (tk, tq), 0)
    t_io = t0 + lax.broadcasted_iota(jnp.int32, (tk, tq), 1)
    diag = t0 // tk
    n_win = -(-(WINDOW - 1) // tk) + 1

    def online_step(carry, s, vt):
        m_old, acc = carry
        m_new = jnp.maximum(m_old, jnp.max(s, axis=0, keepdims=True))
        pe = jnp.exp2(s - m_new).astype(BF16)
        return m_new, jnp.exp2(m_old - m_new) * acc + _dot(vt, pe)

    def bounded_step(carry, s, vt):
        return carry[0], carry[1] + _dot(vt, jnp.exp2(s).astype(BF16))

    def tile_step(kt, carries, window, causal=False, tail=True, span=1, online_step=online_step):
        k0 = pl.multiple_of(kt * tk, tk)
        kpos = k0 + k_io
        out = []
        for g in groups:
            c_slc, c_swa = carries[g]
            if window:
                keys = jnp.concatenate([ks_ref[0, g, pl.ds(k0, tk), :], kw_ref[0, g, pl.ds(k0, tk), :]], axis=0)
                s2 = _dot(keys, q_aug[g])
                s_slc, s_swa = s2[:tk], s2[tk:]
                if causal:
                    s_slc = s_slc + tile_heads(jnp.where(kpos <= t_io, 0.0, NEG))
                if causal and tail:
                    s_swa = s_swa + tile_heads(jnp.where((kpos <= t_io) & (kpos > t_io - WINDOW), 0.0, NEG))
                elif causal:
                    s_swa = s_swa + tile_heads(jnp.where(kpos <= t_io, 0.0, NEG))
                elif tail:
                    s_swa = s_swa + tile_heads(jnp.where(kpos > t_io - WINDOW, 0.0, NEG))
                c_slc = online_step(c_slc, s_slc, vst_ref[0, g, :, pl.ds(k0, tk)])
                c_swa = online_step(c_swa, s_swa, vwt_ref[0, g, :, pl.ds(k0, tk)])
            else:
                s = _dot(ks_ref[0, g, pl.ds(k0, span * tk), :], q_aug[g])
                c_slc = online_step(c_slc, s, vst_ref[0, g, :, pl.ds(k0, span * tk)])
            out.append((c_slc, c_swa))
        return tuple(out)

    rest = jnp.maximum(diag - (n_win - 1), 0)

    def attend(step):
        carries = tile_step(diag, init, True, causal=True, tail=tk - 1 >= WINDOW, online_step=step)
        carries = lax.fori_loop(
            1, jnp.minimum(n_win, diag + 1), lambda i, c: tile_step(diag - i, c, True, online_step=step), carries)
        carries = lax.fori_loop(
            0, rest // 2, lambda p, c: tile_step(rest - 2 - 2 * p, c, False, span=2, online_step=step), carries)
        carries = lax.cond(rest % 2 == 1, lambda c: tile_step(0, c, False, online_step=step), lambda c: c, carries)

        grp_w = NSA_GROUP * NSA_DIM
        for g in groups:
            (_, acc_s), (_, acc_w) = carries[g]
            o_slc = acc_s[:NSA_DIM] / acc_s[NSA_DIM:NSA_DIM + 1]
            o_swa = acc_w[:NSA_DIM] / acc_w[NSA_DIM:NSA_DIM + 1]
            gate = _sigmoid(gate_ref[:, g * LANES:(g + 1) * LANES].T)
            outs = []
            for h in range(NSA_GROUP):
                ls = slice(h * tq, (h + 1) * tq)
                g0 = gate[N_BRANCH * h:N_BRANCH * h + 1, :]
                g1 = gate[N_BRANCH * h + 1:N_BRANCH * h + 2, :]
                g2 = gate[N_BRANCH * h + 2:N_BRANCH * h + 3, :]
                outs.append(g0 * o_cmps[g][:, ls] + g1 * o_slc[:, ls] + g2 * o_swa[:, ls])
            o_ref[:, g * grp_w:(g + 1) * grp_w] = jnp.concatenate(outs, axis=0).T.astype(o_ref.dtype)

    bounded = bound_ok_ref[0] != 0

    @pl.when(bounded)
    def _():
        attend(bounded_step)

    @pl.when(jnp.logical_not(bounded))
    def _():
        attend(online_step)


def _score_bound_ok(q_gain, k_gain):
    k_max = jnp.max(jnp.abs(k_gain[1:]))
    bound = NSA_DIM * jnp.max(jnp.abs(q_gain)) * k_max * (NSA_DIM ** -0.5 * LOG2_E)
    return (bound <= ATT_SAFE_SCORE).astype(jnp.int32).reshape(1)


def _nsa_attn(qt, kc, vct, ks, vst, kw, vwt, gate, bound_ok, batch, seq):
    tq = min(ATT_Q, seq)
    nq = seq // tq
    nseg = seq // CMP_STRIDE
    nslc = seq // SLC_LEN
    top = min(SLC_TOP, nslc)
    c_start = np.arange(nseg) * CMP_STRIDE
    c_end = c_start + CMP_LEN - 1
    s_start = np.arange(nslc) * SLC_LEN
    s_end = s_start + SLC_LEN - 1
    overlap = (c_start[None, :] <= s_end[:, None]) & (c_end[None, :] >= s_start[:, None])
    overlap[:, nseg - 1] = False
    mcs = jnp.asarray(overlap.astype(np.float32), BF16)
    k_full = pl.BlockSpec((1, NSA_KV_HEADS, seq, K_AUG), lambda b, i: (b, 0, 0, 0))
    vt_full = pl.BlockSpec((1, NSA_KV_HEADS, V_AUG, seq), lambda b, i: (b, 0, 0, 0))
    return pl.pallas_call(
        functools.partial(_nsa_attn_kernel, seq=seq, top=top),
        grid=(batch, nq),
        in_specs=[
            pl.BlockSpec((1, NSA_HEADS, NSA_DIM, tq), lambda b, i: (b, 0, 0, i)),
            pl.BlockSpec((1, NSA_KV_HEADS, nseg, NSA_DIM), lambda b, i: (b, 0, 0, 0)),
            pl.BlockSpec((1, NSA_KV_HEADS, NSA_DIM, nseg), lambda b, i: (b, 0, 0, 0)),
            k_full, vt_full, k_full, vt_full,
            pl.BlockSpec((tq, NSA_KV_HEADS * LANES), lambda b, i: (b * nq + i, 0)),
            pl.BlockSpec((nslc, nseg), lambda b, i: (0, 0)),
            pl.BlockSpec(memory_space=pltpu.SMEM),
        ],
        out_specs=pl.BlockSpec((tq, NSA_WIDTH), lambda b, i: (b * nq + i, 0)),
        out_shape=jax.ShapeDtypeStruct((batch * seq, NSA_WIDTH), BF16),
        compiler_params=_params(("parallel", "arbitrary")),
        name="nsa_attn",
    )(qt, kc, vct, ks, vst, kw, vwt, gate, mcs, bound_ok)


def _out_ffn_kernel(h_ref, ohg_ref, onsa_ref, wout_ref, gain_ref, wg_ref, wu_ref, wo_ref,
                    o_ref, xn_ref, acc_ref):
    j = pl.program_id(1)

    @pl.when(j == 0)
    def _():
        h1 = (h_ref[...] + _dot(ohg_ref[...], wout_ref[:HG_WIDTH, :])
              + _dot(onsa_ref[...], wout_ref[HG_WIDTH:, :]))
        acc_ref[...] = h1
        ms = jnp.mean(h1 * h1, axis=-1, keepdims=True)
        xn_ref[...] = (h1 * lax.rsqrt(ms + RMS_EPS) * gain_ref[...]).astype(BF16)

    xn = xn_ref[...]
    act = (_silu(_dot(xn, wg_ref[...])) * _dot(xn, wu_ref[...])).astype(BF16)
    acc_ref[...] += _dot(act, wo_ref[...])

    @pl.when(j == pl.num_programs(1) - 1)
    def _():
        o_ref[...] = acc_ref[...]


def _out_ffn(h, o_hg, o_nsa, w_out, gain, w_ffn_in, w_ffn_out):
    n = h.shape[0]
    tm = PROJ_ROWS
    nff = D_FF // FF_TILE
    mixw = HG_WIDTH + NSA_WIDTH
    return pl.pallas_call(
        _out_ffn_kernel,
        grid=(n // tm, nff),
        in_specs=[
            pl.BlockSpec((tm, D_MODEL), lambda i, j: (i, 0)),
            pl.BlockSpec((tm, HG_WIDTH), lambda i, j: (i, 0)),
            pl.BlockSpec((tm, NSA_WIDTH), lambda i, j: (i, 0)),
            pl.BlockSpec((mixw, D_MODEL), lambda i, j: (0, 0)),
            pl.BlockSpec((1, D_MODEL), lambda i, j: (0, 0)),
            pl.BlockSpec((D_MODEL, FF_TILE), lambda i, j: (0, j)),
            pl.BlockSpec((D_MODEL, FF_TILE), lambda i, j: (0, nff + j)),
            pl.BlockSpec((FF_TILE, D_MODEL), lambda i, j: (j, 0)),
        ],
        out_specs=pl.BlockSpec((tm, D_MODEL), lambda i, j: (i, 0)),
        out_shape=jax.ShapeDtypeStruct((n, D_MODEL), F32),
        scratch_shapes=[pltpu.VMEM((tm, D_MODEL), BF16), pltpu.VMEM((tm, D_MODEL), F32)],
        compiler_params=_params(("parallel", "arbitrary")),
        name="out_ffn",
    )(h, o_hg, o_nsa, w_out, gain, w_ffn_in, w_ffn_in, w_ffn_out)


def _pad_w_in(w):
    pieces = [w[:, :_KV_END]]
    for g in range(NSA_KV_HEADS):
        cols = w[:, _KV_END + g * GATES_PER_GROUP:_KV_END + (g + 1) * GATES_PER_GROUP]
        pieces.append(jnp.pad(cols, ((0, 0), (0, LANES - GATES_PER_GROUP))))
    return jnp.concatenate(pieces, axis=1).astype(BF16)


def kernel(x, w_in, w_out, hg_lb_logits, hg_gnorm, q_gain, k_gain, cmp_pe_k, cmp_w1_k, cmp_w2_k,
           cmp_pe_v, cmp_w1_v, cmp_w2_v, w_ffn_in, w_ffn_out, norm_mix, norm_ffn):
    batch, seq, _ = x.shape
    depth = w_in.shape[0]
    h = x.reshape(batch * seq, D_MODEL)
    for l in range(depth):
        hgp, kc_tok, vc_tok, gate, qt, ks, vst, kw, vwt = _norm_proj(
            h, norm_mix[l][None, :], _pad_w_in(w_in[l]), q_gain[l], k_gain[l], batch, seq)
        o_hg = _hgrn(hgp, hg_lb_logits, hg_gnorm[l][None, :], l, batch, seq)
        kc, vct = _nsa_cmp(kc_tok, vc_tok, cmp_pe_k[l], cmp_w1_k[l], cmp_w2_k[l],
                           cmp_pe_v[l], cmp_w1_v[l], cmp_w2_v[l], k_gain[l, 0], batch, seq)
        o_nsa = _nsa_attn(qt, kc, vct, ks, vst, kw, vwt, gate, _score_bound_ok(q_gain[l], k_gain[l]), batch, seq)
        h = _out_ffn(h, o_hg, o_nsa, w_out[l].astype(BF16), norm_ffn[l][None, :],
                     w_ffn_in[l].astype(BF16), w_ffn_out[l].astype(BF16))
    return h.reshape(batch, seq, D_MODEL)
```

```python
import functools

import jax
import jax.numpy as jnp
import numpy as np
from jax import lax
from jax.experimental import pallas as pl
from jax.experimental.pallas import tpu as pltpu

F32 = jnp.float32
BF16 = jnp.bfloat16

D_MODEL = 1024
HG_HEADS = 4
HG_DIM = 128
HG_WIDTH = HG_HEADS * HG_DIM
NSA_HEADS = 8
NSA_KV_HEADS = 2
NSA_GROUP = NSA_HEADS // NSA_KV_HEADS
NSA_DIM = 64
NSA_WIDTH = NSA_HEADS * NSA_DIM
NSA_KV_WIDTH = NSA_KV_HEADS * NSA_DIM
CMP_LEN = 32
CMP_STRIDE = 16
CMP_HIDDEN = 256
SLC_LEN = 64
SLC_TOP = 16
WINDOW = 512
N_BRANCH = 3
D_FF = 2816
ROPE_THETA = 10000.0
RMS_EPS = 1e-6
NEG = -1e30
FORCED_SCORE = 1e9
LOG2_E = 1.4426950408889634

LANES = 128
V7X_VMEM_LIMIT = 56 * 1024 * 1024

PROJ_ROWS = 512
HG_ROWS = 256
HG_SEQS = 4
HG_CHUNK = 32
HG_SAFE_DECAY = 60.0
ATT_Q = 256
ATT_K = 256
ATT_SAFE_SCORE = 100.0
FF_TILE = 1408
K_AUG = 2 * NSA_DIM
V_AUG = NSA_DIM + 16

_HG_END = 4 * HG_WIDTH
_NQ_END = _HG_END + NSA_WIDTH
_KC_END = _NQ_END + NSA_KV_WIDTH
_VC_END = _KC_END + NSA_KV_WIDTH
_KV_END = _VC_END + 4 * NSA_KV_WIDTH
_GATE_END = _KV_END + NSA_KV_HEADS * LANES
GATES_PER_GROUP = NSA_GROUP * N_BRANCH


def _dot(a, b):
    return jnp.dot(a, b, preferred_element_type=F32)


def _dot_nt(a, b):
    return lax.dot_general(a, b, (((1,), (1,)), ((), ())), preferred_element_type=F32)


def _split3(x):
    hi = x.astype(BF16)
    r1 = x - hi.astype(F32)
    mid = r1.astype(BF16)
    lo = (r1 - mid.astype(F32)).astype(BF16)
    return hi, mid, lo


def _sigmoid(x):
    return 1.0 / (1.0 + jnp.exp(-x))


def _silu(x):
    return x * _sigmoid(x)


def _params(semantics):
    return pltpu.CompilerParams(dimension_semantics=semantics, vmem_limit_bytes=V7X_VMEM_LIMIT)


def _norm_proj_kernel(x_ref, gain_ref, w_ref, qg_ref, kg_ref, cos_ref, sin_ref, bd_ref,
                      hg_ref, kseg_ref, vseg_ref, gate_ref, qt_out, ks_out, vst_out, kw_out, vwt_out,
                      kc_scr, vc_scr):
    x = x_ref[...]
    ms = jnp.mean(x * x, axis=-1, keepdims=True)
    xn = (x * lax.rsqrt(ms + RMS_EPS) * gain_ref[...]).astype(BF16)
    _nsa_prep(_dot(xn, w_ref[:, _HG_END:_NQ_END]), _dot(xn, w_ref[:, _VC_END:_KV_END]),
              qg_ref, kg_ref, cos_ref, sin_ref, bd_ref, qt_out, ks_out, vst_out, kw_out, vwt_out)
    kc_scr[...] = _dot(xn, w_ref[:, _NQ_END:_KC_END])
    vc_scr[...] = _dot(xn, w_ref[:, _KC_END:_VC_END])
    nseg = kseg_ref.shape[0]
    for tok in range(CMP_STRIDE):
        cols = slice(tok * NSA_KV_WIDTH, (tok + 1) * NSA_KV_WIDTH)
        kseg_ref[:, cols] = kc_scr[pl.ds(tok, nseg, stride=CMP_STRIDE), :]
        vseg_ref[:, cols] = vc_scr[pl.ds(tok, nseg, stride=CMP_STRIDE), :]
    gate_ref[...] = _dot(xn, w_ref[:, _KV_END:_GATE_END])
    hg_ref[...] = _dot(xn, w_ref[:, :_HG_END])


def _norm_proj(h, gain, w_pad, q_gain, k_gain, batch, seq):
    n = h.shape[0]
    rows = min(PROJ_ROWS, seq)
    nt = seq // rows
    assert seq // SLC_LEN <= NSA_DIM, "selection one-hot must fit the spare key lanes"
    cos_t, sin_t = _rope_tables(jnp.arange(seq), NSA_HEADS)
    gid = np.arange(NSA_WIDTH) // NSA_DIM
    bd = jnp.asarray((gid[:, None] == gid[None, :]).astype(np.float32), BF16)
    qg = jnp.tile(q_gain, NSA_HEADS)[None, :]
    kg = jnp.stack([jnp.tile(k_gain[1], NSA_KV_HEADS), jnp.tile(k_gain[2], NSA_KV_HEADS)])
    segw = CMP_STRIDE * NSA_KV_WIDTH
    f32_outs = ((rows, n, _HG_END), (rows // CMP_STRIDE, n // CMP_STRIDE, segw),
                (rows // CMP_STRIDE, n // CMP_STRIDE, segw), (rows, n, NSA_KV_HEADS * LANES))
    k_shape =jax.ShapeDtypeStruct((batch, NSA_KV_HEADS, seq, K_AUG), BF16)
    k_spec = pl.BlockSpec((1, NSA_KV_HEADS, rows, K_AUG), lambda b, c: (b, 0, c, 0))
    vt_shape = jax.ShapeDtypeStruct((batch, NSA_KV_HEADS, V_AUG, seq), BF16)
    vt_spec = pl.BlockSpec((1, NSA_KV_HEADS, V_AUG, rows), lambda b, c: (b, 0, 0, c))

    def const(shape):
        return pl.BlockSpec(shape, lambda b, c: (0,) * len(shape))

    return pl.pallas_call(
        _norm_proj_kernel,
        grid=(batch, nt),
        in_specs=[
            pl.BlockSpec((rows, D_MODEL), lambda b, c: (b * nt + c, 0)),
            const((1, D_MODEL)),
            const((D_MODEL, _GATE_END)),
            const((1, NSA_WIDTH)),
            const((2, NSA_KV_WIDTH)),
            pl.BlockSpec((rows, NSA_WIDTH), lambda b, c: (c, 0)),
            pl.BlockSpec((rows, NSA_WIDTH), lambda b, c: (c, 0)),
            const((NSA_WIDTH, NSA_WIDTH)),
        ],
        out_specs=[pl.BlockSpec((r, w), lambda b, c: (b * nt + c, 0)) for r, _, w in f32_outs] + [
            pl.BlockSpec((1, NSA_HEADS, NSA_DIM, rows), lambda b, c: (b, 0, 0, c)),
            k_spec, vt_spec, k_spec, vt_spec,
        ],
        out_shape=[jax.ShapeDtypeStruct((total, w), F32) for _, total, w in f32_outs] + [
            jax.ShapeDtypeStruct((batch, NSA_HEADS, NSA_DIM, seq), BF16),
            k_shape, vt_shape, k_shape, vt_shape,
        ],
        scratch_shapes=[pltpu.VMEM((rows, NSA_KV_WIDTH), F32), pltpu.VMEM((rows, NSA_KV_WIDTH), F32)],
        compiler_params=_params(("parallel", "parallel")),
        name="norm_proj",
    )(h, gain, w_pad, qg, kg, cos_t, sin_t, bd)


def _hgrn_kernel(q_seqs, f_seqs, i_seqs, g_seqs, lbl_ref, gn_ref, tri_ref, o_seqs,
                 s_seqs, b_seqs, k_seqs, *, layer, depth):
    nseq, rows = q_seqs.shape[0], q_seqs.shape[1]
    n_chunks = rows // HG_CHUNK
    c = HG_CHUNK

    @pl.when(pl.program_id(1) == 0)
    def _():
        s_seqs[...] = jnp.zeros_like(s_seqs)

    lg = lbl_ref[...]
    rows_l = [lg[i:i + 1, :] for i in range(depth)]
    mx = functools.reduce(jnp.maximum, rows_l)
    ex = [jnp.exp(r - mx) for r in rows_l]
    den = functools.reduce(lambda a, b_: a + b_, ex)
    pr = [e / den for e in ex]
    lb = functools.reduce(lambda a, b_: a + b_, pr[:layer + 1]) - pr[0]

    def gates(f_ref):
        f = lb + (1.0 - lb) * _sigmoid(f_ref[...])
        w = f.shape[1]
        b3 = _dot(tri_ref[...], jnp.concatenate(_split3(jnp.log(f)), axis=1))
        return 1.0 - f, b3[:, :w] + b3[:, w:2 * w] + b3[:, 2 * w:]

    gated = [gates(f_seqs.at[e]) for e in range(nseq)]
    needs_exact = functools.reduce(jnp.minimum, [jnp.min(b) for _, b in gated]) < -HG_SAFE_DECAY
    gn = gn_ref[...]

    def make_finish(o_ref, g_ref):
        def finish(o, rows_sl, cs):
            ms = jnp.mean(o * o, axis=-1, keepdims=True)
            y = o * lax.rsqrt(ms + RMS_EPS) * gn
            o_ref[rows_sl, cs] = (y * _silu(g_ref[rows_sl, cs])).astype(o_ref.dtype)
        return finish

    def fast_path(q_ref, i_ref, g_ref, o_ref, s_ref, kk, b_all):
        finish = make_finish(o_ref, g_ref)
        nb = n_chunks // 2
        blk = 2 * c
        qf = _silu(q_ref[...])
        qt_l, kt_l, kh_l, qi_l, kb_l, dec = [], [], [], [], [], []
        for j in range(nb):
            parts = []
            for half in range(2):
                rs = slice((2 * j + half) * c, (2 * j + half + 1) * c)
                b = b_all[rs]
                tot = b[c - 1:c, :]
                parts.append((qf[rs] * jnp.exp(b), kk[rs] * jnp.exp(-b), kk[rs] * jnp.exp(tot - b), jnp.exp(tot)))
            (qa, ka, ha, ea), (qb, kb, hb, eb) = parts
            qt_l += [qa, qb]
            kt_l += [ka, kb]
            kh_l += [ha, hb]
            qi_l.append(jnp.concatenate([qa, qb * ea], axis=0))
            kb_l.append(jnp.concatenate([ha * eb, hb], axis=0))
            dec.append(ea * eb)

        def span_decay(lo, hi):
            return functools.reduce(lambda x, y: x * y, dec[lo:hi]) if hi > lo else None

        def scaled(x, d):
            return x if d is None else x * d

        qt = jnp.concatenate(qt_l, axis=0).astype(BF16)
        k_in = jnp.concatenate(kt_l + kh_l, axis=0).astype(BF16)
        kb_all = jnp.concatenate(kb_l, axis=0).astype(BF16)
        q_back = jnp.concatenate(
            [scaled(qi_l[j], span_decay(max(j - dist + 1, 0), j)) for dist in range(1, nb) for j in range(nb)],
            axis=0).astype(BF16)
        q_state = jnp.concatenate([scaled(qi_l[j], span_decay(0, j)) for j in range(nb)], axis=0).astype(BF16)
        k_state = jnp.concatenate([scaled(kb_l[j], span_decay(j + 1, nb)) for j in range(nb)], axis=0).astype(BF16)
        dec_all = span_decay(0, nb)
        t_io = lax.broadcasted_iota(jnp.int32, (rows, rows), 0)
        s_io = lax.broadcasted_iota(jnp.int32, (rows, rows), 1)
        same_chunk = (t_io // c == s_io // c) & (s_io <= t_io)
        cross = (t_io // blk == s_io // blk) & (t_io // c > s_io // c)
        back = [t_io // blk - s_io // blk == dist for dist in range(1, nb)]

        def head(h):
            cs = slice(h * HG_DIM, (h + 1) * HG_DIM)
            v = i_ref[:, cs]
            near = _dot_nt(qt[:, cs], k_in[:, cs])
            a = jnp.where(same_chunk, near[:, :rows], 0.0) + jnp.where(cross, near[:, rows:], 0.0)
            if nb > 1:
                far = _dot_nt(q_back[:, cs], kb_all[:, cs])
                for i, m in enumerate(back):
                    a = a + jnp.where(m, far[i * rows:(i + 1) * rows], 0.0)
            st = s_ref[h]
            o = _dot(a.astype(BF16), v.astype(BF16)) + _dot_nt(q_state[:, cs], st.astype(BF16))
            s_ref[h] = st * dec_all[:, cs] + _dot(v.T.astype(BF16), k_state[:, cs])
            finish(o, slice(None), cs)

        return head

    def exact_path(q_ref, i_ref, g_ref, o_ref, s_ref, b_ref, k_ref, kk, b_all):
        finish = make_finish(o_ref, g_ref)
        b_ref[...] = b_all
        k_ref[...] = kk
        t_io = lax.broadcasted_iota(jnp.int32, (c, c), 0)
        s_io = lax.broadcasted_iota(jnp.int32, (c, c), 1)
        row_io = lax.broadcasted_iota(jnp.int32, (c, HG_DIM), 0)

        def chunk_step(ci, _):
            r0 = pl.multiple_of(ci * c, c)
            for h in range(HG_HEADS):
                cs = slice(h * HG_DIM, (h + 1) * HG_DIM)
                qf = _silu(q_ref[pl.ds(r0, c), cs])
                b = b_ref[pl.ds(r0, c), cs]
                k = k_ref[pl.ds(r0, c), cs]
                v = i_ref[pl.ds(r0, c), cs]
                b_last = b[c - 1:c, :]
                a = jnp.zeros((c, c), F32)
                for s in range(c):
                    d = jnp.where(row_io >= s, b - b[s:s + 1, :], NEG)
                    col = jnp.sum(qf * k[s:s + 1, :] * jnp.exp(d), axis=-1, keepdims=True)
                    a = jnp.where(s_io == s, col, a)
                st = s_ref[h]
                o = _dot(a.astype(BF16), v.astype(BF16)) + _dot_nt((qf * jnp.exp(b)).astype(BF16), st.astype(BF16))
                s_ref[h] = st * jnp.exp(b_last) + _dot(v.T.astype(BF16), (k * jnp.exp(b_last - b)).astype(BF16))
                finish(o, pl.ds(r0, c), cs)

        lax.fori_loop(0, n_chunks, chunk_step, None)

    @pl.when(jnp.logical_not(needs_exact))
    def _():
        heads = [fast_path(q_seqs.at[e], i_seqs.at[e], g_seqs.at[e], o_seqs.at[e], s_seqs.at[e], *gated[e])
                 for e in range(nseq)]
        for h in range(HG_HEADS):
            for e in range(nseq):
                heads[e](h)

    @pl.when(needs_exact)
    def _():
        for e in range(nseq):
            exact_path(q_seqs.at[e], i_seqs.at[e], g_seqs.at[e], o_seqs.at[e], s_seqs.at[e],
                       b_seqs.at[e], k_seqs.at[e], *gated[e])


def _hgrn(hgp, lb_logits, gnorm, layer, batch, seq):
    rows = min(HG_ROWS, seq)
    nt = seq // rows
    nseq = HG_SEQS if batch % HG_SEQS == 0 else 1
    depth = lb_logits.shape[0]
    idx = np.arange(rows)
    tri = ((idx[:, None] >= idx[None, :]) & (idx[:, None] // HG_CHUNK == idx[None, :] // HG_CHUNK))
    tri = jnp.asarray(tri.astype(np.float32), BF16)
    hg3 = hgp.reshape(batch, seq, 4 * HG_WIDTH)

    def col_spec(j):
        return pl.BlockSpec((nseq, rows, HG_WIDTH), lambda b, c, j=j: (b, c, j))

    out = pl.pallas_call(
        functools.partial(_hgrn_kernel, layer=layer, depth=depth),
        grid=(batch // nseq, nt),
        in_specs=[
            col_spec(0), col_spec(1), col_spec(2), col_spec(3),
            pl.BlockSpec((depth, HG_WIDTH), lambda b, c: (0, 0)),
            pl.BlockSpec((1, HG_DIM), lambda b, c: (0, 0)),
            pl.BlockSpec((rows, rows), lambda b, c: (0, 0)),
        ],
        out_specs=pl.BlockSpec((nseq, rows, HG_WIDTH), lambda b, c: (b, c, 0)),
        out_shape=jax.ShapeDtypeStruct((batch, seq, HG_WIDTH), BF16),
        scratch_shapes=[
            pltpu.VMEM((nseq, HG_HEADS, HG_DIM, HG_DIM), F32),
            pltpu.VMEM((nseq, rows, HG_WIDTH), F32),
            pltpu.VMEM((nseq, rows, HG_WIDTH), F32),
        ],
        compiler_params=_params(("parallel", "arbitrary")),
        name="hgrn",
    )(hg3, hg3, hg3, hg3, lb_logits, gnorm, tri)
    return out.reshape(batch * seq, HG_WIDTH)


def _group_rms(x, bd, gain):
    sq = x * x
    hi = sq.astype(BF16)
    lo = (sq - hi.astype(F32)).astype(BF16)
    ss = _dot(hi, bd) + _dot(lo, bd)
    return x * lax.rsqrt(ss * (1.0 / NSA_DIM) + RMS_EPS) * gain


def _rope_lanes(x, cos, sin_signed):
    w = x.shape[-1]
    half = NSA_DIM // 2
    lane = lax.broadcasted_iota(jnp.int32, x.shape, 1)
    first = (lane & (NSA_DIM - 1)) < half
    swapped = jnp.where(first, pltpu.roll(x, w - half, 1), pltpu.roll(x, half, 1))
    return x * cos + swapped * sin_signed


def _nsa_prep(nq, kv, qg_ref, kg_ref, cos_ref, sin_ref, bd_ref, qt_out, ks_out, vst_out, kw_out, vwt_out):
    cos = cos_ref[...]
    sin = sin_ref[...]
    bd = bd_ref[...]
    kvw = NSA_KV_WIDTH
    rows = nq.shape[0]
    q = _rope_lanes(_group_rms(nq, bd, qg_ref[...]), cos, sin) * (NSA_DIM ** -0.5 * LOG2_E)
    qt_out[0] = q.T.reshape(NSA_HEADS, NSA_DIM, rows).astype(BF16)
    bdk = bd[:kvw, :kvw]
    cosk = cos[:, :kvw]
    sink = sin[:, :kvw]
    ks = _rope_lanes(_group_rms(kv[:, 0:kvw], bdk, kg_ref[0:1, :]), cosk, sink)
    kw = _rope_lanes(_group_rms(kv[:, 2 * kvw:3 * kvw], bdk, kg_ref[1:2, :]), cosk, sink)
    pad_rows = V_AUG - NSA_DIM
    ones_row = jnp.where(lax.broadcasted_iota(jnp.int32, (NSA_KV_HEADS, pad_rows, rows), 1) == 0, 1.0, 0.0)

    def values_t(v):
        vt = v.T.reshape(NSA_KV_HEADS, NSA_DIM, rows)
        return jnp.concatenate([vt, ones_row], axis=1).astype(BF16)

    vst_out[0] = values_t(kv[:, kvw:2 * kvw])
    vwt_out[0] = values_t(kv[:, 3 * kvw:4 * kvw])
    tok = pl.program_id(1) * rows + lax.broadcasted_iota(jnp.int32, (rows, NSA_DIM), 0)
    onehot = jnp.where(lax.broadcasted_iota(jnp.int32, (rows, NSA_DIM), 1) == tok // SLC_LEN, 1.0, 0.0)
    zeros = jnp.zeros((rows, NSA_DIM), F32)
    for g in range(NSA_KV_HEADS):
        sl = slice(g * NSA_DIM, (g + 1) * NSA_DIM)
        ks_out[0, g] = jnp.concatenate([ks[:, sl], onehot], axis=1).astype(BF16)
        kw_out[0, g] = jnp.concatenate([kw[:, sl], zeros], axis=1).astype(BF16)


def _rope_tables(pos, reps):
    half = NSA_DIM // 2
    freqs = ROPE_THETA ** (-jnp.arange(half, dtype=F32) / half)
    ang = pos.astype(F32)[:, None] * freqs[None, :]
    cos = jnp.cos(ang)
    sin = jnp.sin(ang)
    cos_t = jnp.tile(jnp.concatenate([cos, cos], axis=1), (1, reps))
    sin_t = jnp.tile(jnp.concatenate([-sin, sin], axis=1), (1, reps))
    return cos_t, sin_t


def _nsa_cmp_kernel(kseg_ref, vseg_ref, pek_ref, pev_ref, w1k_ref, w1v_ref, w2k_ref, w2v_ref,
                    kg_ref, cos_ref, sin_ref, kc_out, vct_out):
    nseg = kseg_ref.shape[1]

    def hidden(seg, pe_ref, w1_ref):
        y0 = _dot((seg + pe_ref[0:1, :]).astype(BF16), w1_ref[0])
        y1 = _dot((seg + pe_ref[1:2, :]).astype(BF16), w1_ref[1])
        act = _silu(y0 + pltpu.roll(y1, nseg - 1, 0)).astype(BF16)
        return [act[:, g * CMP_HIDDEN:(g + 1) * CMP_HIDDEN] for g in range(NSA_KV_HEADS)]

    hk = hidden(kseg_ref[0], pek_ref, w1k_ref)
    hv = hidden(vseg_ref[0], pev_ref, w1v_ref)
    half = NSA_DIM // 2
    for g in range(NSA_KV_HEADS):
        x = _dot(hk[g], w2k_ref[...])
        ms = jnp.mean(x * x, axis=-1, keepdims=True)
        xn = x * lax.rsqrt(ms + RMS_EPS) * kg_ref[...]
        swapped = jnp.concatenate([xn[:, half:], xn[:, :half]], axis=1)
        kc_out[0, g] = (xn * cos_ref[...] + swapped * sin_ref[...]).astype(BF16)
        vct_out[0, g] = _dot_nt(w2v_ref[...], hv[g]).astype(BF16)


def _cmp_weights(pe, w1):
    pe_r = pe.reshape(2, CMP_STRIDE, 1, NSA_DIM)
    pe_seg = jnp.broadcast_to(pe_r, (2, CMP_STRIDE, NSA_KV_HEADS, NSA_DIM)).reshape(2, CMP_STRIDE * NSA_KV_WIDTH)
    w1r = w1.reshape(2, CMP_STRIDE, NSA_DIM, CMP_HIDDEN)
    eye = jnp.eye(NSA_KV_HEADS, dtype=w1.dtype)
    wbig = jnp.einsum('rldj,gh->rlgdhj', w1r, eye)
    wbig = wbig.reshape(2, CMP_STRIDE * NSA_KV_WIDTH, NSA_KV_HEADS * CMP_HIDDEN)
    return pe_seg, wbig.astype(BF16)


def _nsa_cmp(kc_seg, vc_seg, pe_k, w1_k, w2_k, pe_v, w1_v, w2_v, k_gain0, batch, seq):
    nseg = seq // CMP_STRIDE
    segw = CMP_STRIDE * NSA_KV_WIDTH
    kseg = kc_seg.reshape(batch, nseg, segw)
    vseg = vc_seg.reshape(batch, nseg, segw)
    pek, w1k = _cmp_weights(pe_k, w1_k)
    pev, w1v = _cmp_weights(pe_v, w1_v)
    cmp_end = jnp.arange(nseg) * CMP_STRIDE + CMP_LEN - 1
    cos_t, sin_t = _rope_tables(cmp_end, 1)
    hid2 = NSA_KV_HEADS * CMP_HIDDEN
    seg_spec = pl.BlockSpec((1, nseg, segw), lambda b: (b, 0, 0))
    out_spec = pl.BlockSpec((1, NSA_KV_HEADS, nseg, NSA_DIM), lambda b: (b, 0, 0, 0))
    out_shape = jax.ShapeDtypeStruct((batch, NSA_KV_HEADS, nseg, NSA_DIM), BF16)
    out_t_spec = pl.BlockSpec((1, NSA_KV_HEADS, NSA_DIM, nseg), lambda b: (b, 0, 0, 0))
    out_t_shape = jax.ShapeDtypeStruct((batch, NSA_KV_HEADS, NSA_DIM, nseg), BF16)
    return pl.pallas_call(
        _nsa_cmp_kernel,
        grid=(batch,),
        in_specs=[
            seg_spec, seg_spec,
            pl.BlockSpec((2, segw), lambda b: (0, 0)),
            pl.BlockSpec((2, segw), lambda b: (0, 0)),
            pl.BlockSpec((2, segw, hid2), lambda b: (0, 0, 0)),
            pl.BlockSpec((2, segw, hid2), lambda b: (0, 0, 0)),
            pl.BlockSpec((CMP_HIDDEN, NSA_DIM), lambda b: (0, 0)),
            pl.BlockSpec((NSA_DIM, CMP_HIDDEN), lambda b: (0, 0)),
            pl.BlockSpec((1, NSA_DIM), lambda b: (0, 0)),
            pl.BlockSpec((nseg, NSA_DIM), lambda b: (0, 0)),
            pl.BlockSpec((nseg, NSA_DIM), lambda b: (0, 0)),
        ],
        out_specs=[out_spec, out_t_spec],
        out_shape=[out_shape, out_t_shape],
        compiler_params=_params(("parallel",)),
        name="nsa_cmp",
    )(kseg, vseg, pek, pev, w1k, w1v, w2_k.astype(BF16), w2_v.T.astype(BF16),
      k_gain0[None, :], cos_t, sin_t)


def _nsa_attn_kernel(qt_ref, kc_ref, vct_ref, ks_ref, vst_ref, kw_ref, vwt_ref, gate_ref,
                     mcs_ref, bound_ok_ref, o_ref, *, seq, top):
    tq = qt_ref.shape[3]
    tk = min(ATT_K, seq)
    nseg = kc_ref.shape[2]
    nslc = mcs_ref.shape[0]
    t0 = pl.program_id(1) * tq
    groups = range(NSA_KV_HEADS)
    qts = [jnp.concatenate([qt_ref[0, g * NSA_GROUP + h] for h in range(NSA_GROUP)], axis=1) for g in groups]

    def tile_heads(x):
        return jnp.concatenate([x] * NSA_GROUP, axis=1)

    blk_n = lax.broadcasted_iota(jnp.int32, (nseg, tq), 0)
    t_col = t0 + lax.broadcasted_iota(jnp.int32, (nseg, tq), 1)
    valid = (blk_n * CMP_STRIDE + (CMP_LEN - 1)) <= t_col
    j_io = lax.broadcasted_iota(jnp.int32, (nslc, tq), 0)
    cur = (t0 + lax.broadcasted_iota(jnp.int32, (nslc, tq), 1)) // SLC_LEN
    forced = (j_io == 0) | (j_io == cur) | (j_io == cur - 1)
    mcs = mcs_ref[...]
    o_cmps = []
    q_aug = []
    for g in groups:
        sc = _dot(kc_ref[0, g], qts[g])
        p_heads = []
        for h in range(NSA_GROUP):
            sm = jnp.where(valid, sc[:, h * tq:(h + 1) * tq], NEG)
            p = jnp.where(valid, jnp.exp2(sm - jnp.max(sm, axis=0, keepdims=True)), 0.0)
            p_heads.append(p / jnp.maximum(jnp.sum(p, axis=0, keepdims=True), 1e-30))
        o_cmps.append(_dot(vct_ref[0, g], jnp.concatenate(p_heads, axis=1).astype(BF16)))

        p_sum = functools.reduce(lambda a, b: a + b, p_heads)
        imp = functools.reduce(lambda a, b: a + b, [_dot(mcs, piece) for piece in _split3(p_sum)])
        score = jnp.where(forced, FORCED_SCORE, jnp.where(j_io <= cur, imp, NEG))
        rank = jnp.zeros((nslc, tq), F32)
        for jp in range(nslc):
            other = score[jp:jp + 1, :]
            ahead = (other > score) | ((other == score) & (j_io > jp))
            rank = rank + jnp.where(ahead, 1.0, 0.0)
        sel_bias = jnp.where(rank < top, 0.0, NEG)
        sel_rows = jnp.concatenate([sel_bias, jnp.zeros((K_AUG - NSA_DIM - nslc, tq), F32)], axis=0)
        q_aug.append(jnp.concatenate([qts[g], tile_heads(sel_rows).astype(BF16)], axis=0))

    lanes = NSA_GROUP * tq
    init1 = (jnp.full((1, lanes), NEG, F32), jnp.zeros((V_AUG, lanes), F32))
    init = tuple((init1, init1) for _ in groups)
    k_io = lax.broadcasted_iota(jnp.int32, (tk, tq), 0)
    t_io = t0 + lax.broadcasted_iota(jnp.int32, (tk, tq), 1)
    diag = t0 // tk
    n_win = -(-(WINDOW - 1) // tk) + 1

    def online_step(carry, s, vt):
        m_old, acc = carry
        m_new = jnp.maximum(m_old, jnp.max(s, axis=0, keepdims=True))
        pe = jnp.exp2(s - m_new).astype(BF16)
        return m_new, jnp.exp2(m_old - m_new) * acc + _dot(vt, pe)

    def bounded_step(carry, s, vt):
        return carry[0], carry[1] + _dot(vt, jnp.exp2(s).astype(BF16))

    def tile_step(kt, carries, window, causal=False, tail=True, span=1, online_step=online_step):
        k0 = pl.multiple_of(kt * tk, tk)
        kpos = k0 + k_io
        out = []
        for g in groups:
            c_slc, c_swa = carries[g]
            if window:
                keys = jnp.concatenate([ks_ref[0, g, pl.ds(k0, tk), :], kw_ref[0, g, pl.ds(k0, tk), :]], axis=0)
                s2 = _dot(keys, q_aug[g])
                s_slc, s_swa = s2[:tk], s2[tk:]
                if causal:
                    s_slc = s_slc + tile_heads(jnp.where(kpos <= t_io, 0.0, NEG))
                if causal and tail:
                    s_swa = s_swa + tile_heads(jnp.where((kpos <= t_io) & (kpos > t_io - WINDOW), 0.0, NEG))
                elif causal:
                    s_swa = s_swa + tile_heads(jnp.where(kpos <= t_io, 0.0, NEG))
                elif tail:
                    s_swa = s_swa + tile_heads(jnp.where(kpos > t_io - WINDOW, 0.0, NEG))
                c_slc = online_step(c_slc, s_slc, vst_ref[0, g, :, pl.ds(k0, tk)])
                c_swa = online_step(c_swa, s_swa, vwt_ref[0, g, :, pl.ds(k0, tk)])
            else:
                s = _dot(ks_ref[0, g, pl.ds(k0, span * tk), :], q_aug[g])
                c_slc = online_step(c_slc, s, vst_ref[0, g, :, pl.ds(k0, span * tk)])
            out.append((c_slc, c_swa))
        return tuple(out)

    rest = jnp.maximum(diag - (n_win - 1), 0)

    def attend(step):
        carries = tile_step(diag, init, True, causal=True, tail=tk - 1 >= WINDOW, online_step=step)
        carries = lax.fori_loop(
            1, jnp.minimum(n_win, diag + 1), lambda i, c: tile_step(diag - i, c, True, online_step=step), carries)
        carries = lax.fori_loop(
            0, rest // 2, lambda p, c: tile_step(rest - 2 - 2 * p, c, False, span=2, online_step=step), carries)
        carries = lax.cond(rest % 2 == 1, lambda c: tile_step(0, c, False, online_step=step), lambda c: c, carries)

        grp_w = NSA_GROUP * NSA_DIM
        for g in groups:
            (_, acc_s), (_, acc_w) = carries[g]
            o_slc = acc_s[:NSA_DIM] / acc_s[NSA_DIM:NSA_DIM + 1]
            o_swa = acc_w[:NSA_DIM] / acc_w[NSA_DIM:NSA_DIM + 1]
            gate = _sigmoid(gate_ref[:, g * LANES:(g + 1) * LANES].T)
            outs = []
            for h in range(NSA_GROUP):
                ls = slice(h * tq, (h + 1) * tq)
                g0 = gate[N_BRANCH * h:N_BRANCH * h + 1, :]
                g1 = gate[N_BRANCH * h + 1:N_BRANCH * h + 2, :]
                g2 = gate[N_BRANCH * h + 2:N_BRANCH * h + 3, :]
                outs.append(g0 * o_cmps[g][:, ls] + g1 * o_slc[:, ls] + g2 * o_swa[:, ls])
            o_ref[:, g * grp_w:(g + 1) * grp_w] = jnp.concatenate(outs, axis=0).T.astype(o_ref.dtype)

    bounded = bound_ok_ref[0] != 0

    @pl.when(bounded)
    def _():
        attend(bounded_step)

    @pl.when(jnp.logical_not(bounded))
    def _():
        attend(online_step)


def _score_bound_ok(q_gain, k_gain):
    k_max = jnp.max(jnp.abs(k_gain[1:]))
    bound = NSA_DIM * jnp.max(jnp.abs(q_gain)) * k_max * (NSA_DIM ** -0.5 * LOG2_E)
    return (bound <= ATT_SAFE_SCORE).astype(jnp.int32).reshape(1)


def _nsa_attn(qt, kc, vct, ks, vst, kw, vwt, gate, bound_ok, batch, seq):
    tq = min(ATT_Q, seq)
    nq = seq // tq
    nseg = seq // CMP_STRIDE
    nslc = seq // SLC_LEN
    top = min(SLC_TOP, nslc)
    c_start = np.arange(nseg) * CMP_STRIDE
    c_end = c_start + CMP_LEN - 1
    s_start = np.arange(nslc) * SLC_LEN
    s_end = s_start + SLC_LEN - 1
    overlap = (c_start[None, :] <= s_end[:, None]) & (c_end[None, :] >= s_start[:, None])
    overlap[:, nseg - 1] = False
    mcs = jnp.asarray(overlap.astype(np.float32), BF16)
    k_full = pl.BlockSpec((1, NSA_KV_HEADS, seq, K_AUG), lambda b, i: (b, 0, 0, 0))
    vt_full = pl.BlockSpec((1, NSA_KV_HEADS, V_AUG, seq), lambda b, i: (b, 0, 0, 0))
    return pl.pallas_call(
        functools.partial(_nsa_attn_kernel, seq=seq, top=top),
        grid=(batch, nq),
        in_specs=[
            pl.BlockSpec((1, NSA_HEADS, NSA_DIM, tq), lambda b, i: (b, 0, 0, i)),
            pl.BlockSpec((1, NSA_KV_HEADS, nseg, NSA_DIM), lambda b, i: (b, 0, 0, 0)),
            pl.BlockSpec((1, NSA_KV_HEADS, NSA_DIM, nseg), lambda b, i: (b, 0, 0, 0)),
            k_full, vt_full, k_full, vt_full,
            pl.BlockSpec((tq, NSA_KV_HEADS * LANES), lambda b, i: (b * nq + i, 0)),
            pl.BlockSpec((nslc, nseg), lambda b, i: (0, 0)),
            pl.BlockSpec(memory_space=pltpu.SMEM),
        ],
        out_specs=pl.BlockSpec((tq, NSA_WIDTH), lambda b, i: (b * nq + i, 0)),
        out_shape=jax.ShapeDtypeStruct((batch * seq, NSA_WIDTH), BF16),
        compiler_params=_params(("parallel", "arbitrary")),
        name="nsa_attn",
    )(qt, kc, vct, ks, vst, kw, vwt, gate, mcs, bound_ok)


def _out_ffn_kernel(h_ref, ohg_ref, onsa_ref, wout_ref, gain_ref, wg_ref, wu_ref, wo_ref,
                    o_ref, xn_ref, acc_ref):
    j = pl.program_id(1)

    @pl.when(j == 0)
    def _():
        h1 = (h_ref[...] + _dot(ohg_ref[...], wout_ref[:HG_WIDTH, :])
              + _dot(onsa_ref[...], wout_ref[HG_WIDTH:, :]))
        acc_ref[...] = h1
        ms = jnp.mean(h1 * h1, axis=-1, keepdims=True)
        xn_ref[...] = (h1 * lax.rsqrt(ms + RMS_EPS) * gain_ref[...]).astype(BF16)

    xn = xn_ref[...]
    act = (_silu(_dot(xn, wg_ref[...])) * _dot(xn, wu_ref[...])).astype(BF16)
    acc_ref[...] += _dot(act, wo_ref[...])

    @pl.when(j == pl.num_programs(1) - 1)
    def _():
        o_ref[...] = acc_ref[...]


def _out_ffn(h, o_hg, o_nsa, w_out, gain, w_ffn_in, w_ffn_out):
    n = h.shape[0]
    tm = PROJ_ROWS
    nff = D_FF // FF_TILE
    mixw = HG_WIDTH + NSA_WIDTH
    return pl.pallas_call(
        _out_ffn_kernel,
        grid=(n // tm, nff),
        in_specs=[
            pl.BlockSpec((tm, D_MODEL), lambda i, j: (i, 0)),
            pl.BlockSpec((tm, HG_WIDTH), lambda i, j: (i, 0)),
            pl.BlockSpec((tm, NSA_WIDTH), lambda i, j: (i, 0)),
            pl.BlockSpec((mixw, D_MODEL), lambda i, j: (0, 0)),
            pl.BlockSpec((1, D_MODEL), lambda i, j: (0, 0)),
            pl.BlockSpec((D_MODEL, FF_TILE), lambda i, j: (0, j)),
            pl.BlockSpec((D_MODEL, FF_TILE), lambda i, j: (0, nff + j)),
            pl.BlockSpec((FF_TILE, D_MODEL), lambda i, j: (j, 0)),
        ],
        out_specs=pl.BlockSpec((tm, D_MODEL), lambda i, j: (i, 0)),
        out_shape=jax.ShapeDtypeStruct((n, D_MODEL), F32),
        scratch_shapes=[pltpu.VMEM((tm, D_MODEL), BF16), pltpu.VMEM((tm, D_MODEL), F32)],
        compiler_params=_params(("parallel", "arbitrary")),
        name="out_ffn",
    )(h, o_hg, o_nsa, w_out, gain, w_ffn_in, w_ffn_in, w_ffn_out)


def _pad_w_in(w):
    pieces = [w[:, :_KV_END]]
    for g in range(NSA_KV_HEADS):
        cols = w[:, _KV_END + g * GATES_PER_GROUP:_KV_END + (g + 1) * GATES_PER_GROUP]
        pieces.append(jnp.pad(cols, ((0, 0), (0, LANES - GATES_PER_GROUP))))
    return jnp.concatenate(pieces, axis=1).astype(BF16)


def kernel(x, w_in, w_out, hg_lb_logits, hg_gnorm, q_gain, k_gain, cmp_pe_k, cmp_w1_k, cmp_w2_k,
           cmp_pe_v, cmp_w1_v, cmp_w2_v, w_ffn_in, w_ffn_out, norm_mix, norm_ffn):
    batch, seq, _ = x.shape
    depth = w_in.shape[0]
    h = x.reshape(batch * seq, D_MODEL)
    for l in range(depth):
        hgp, kc_tok, vc_tok, gate, qt, ks, vst, kw, vwt = _norm_proj(
            h, norm_mix[l][None, :], _pad_w_in(w_in[l]), q_gain[l], k_gain[l], batch, seq)
        o_hg = _hgrn(hgp, hg_lb_logits, hg_gnorm[l][None, :], l, batch, seq)
        kc, vct = _nsa_cmp(kc_tok, vc_tok, cmp_pe_k[l], cmp_w1_k[l], cmp_w2_k[l],
                           cmp_pe_v[l], cmp_w1_v[l], cmp_w2_v[l], k_gain[l, 0], batch, seq)
        o_nsa = _nsa_attn(qt, kc, vct, ks, vst, kw, vwt, gate, _score_bound_ok(q_gain[l], k_gain[l]), batch, seq)
        h = _out_ffn(h, o_hg, o_nsa, w_out[l].astype(BF16), norm_ffn[l][None, :],
                     w_ffn_in[l].astype(BF16), w_ffn_out[l].astype(BF16))
    return h.reshape(batch, seq, D_MODEL)
```

```python
import functools

import jax
import jax.numpy as jnp
import numpy as np
from jax import lax
from jax.experimental import pallas as pl
from jax.experimental.pallas import tpu as pltpu

F32 = jnp.float32
BF16 = jnp.bfloat16

D_MODEL = 1024
HG_HEADS = 4
HG_DIM = 128
HG_WIDTH = HG_HEADS * HG_DIM
NSA_HEADS = 8
NSA_KV_HEADS = 2
NSA_GROUP = NSA_HEADS // NSA_KV_HEADS
NSA_DIM = 64
NSA_WIDTH = NSA_HEADS * NSA_DIM
NSA_KV_WIDTH = NSA_KV_HEADS * NSA_DIM
CMP_LEN = 32
CMP_STRIDE = 16
CMP_HIDDEN = 256
SLC_LEN = 64
SLC_TOP = 16
WINDOW = 512
N_BRANCH = 3
D_FF = 2816
ROPE_THETA = 10000.0
RMS_EPS = 1e-6
NEG = -1e30
FORCED_SCORE = 1e9
LOG2_E = 1.4426950408889634

LANES = 128
V7X_VMEM_LIMIT = 56 * 1024 * 1024

PROJ_ROWS = 512
HG_ROWS = 256
HG_SEQS = 4
HG_CHUNK = 32
HG_SAFE_DECAY = 60.0
ATT_Q = 256
ATT_K = 256
ATT_SEQS = 2
ATT_SAFE_SCORE = 100.0
FF_TILE = 1408
K_AUG = 2 * NSA_DIM
V_AUG = NSA_DIM + 16

_HG_END = 4 * HG_WIDTH
_NQ_END = _HG_END + NSA_WIDTH
_KC_END = _NQ_END + NSA_KV_WIDTH
_VC_END = _KC_END + NSA_KV_WIDTH
_KV_END = _VC_END + 4 * NSA_KV_WIDTH
_GATE_END = _KV_END + NSA_KV_HEADS * LANES
GATES_PER_GROUP = NSA_GROUP * N_BRANCH


def _dot(a, b):
    return jnp.dot(a, b, preferred_element_type=F32)


def _dot_nt(a, b):
    return lax.dot_general(a, b, (((1,), (1,)), ((), ())), preferred_element_type=F32)


def _split3(x):
    hi = x.astype(BF16)
    r1 = x - hi.astype(F32)
    mid = r1.astype(BF16)
    lo = (r1 - mid.astype(F32)).astype(BF16)
    return hi, mid, lo


def _sigmoid(x):
    return 1.0 / (1.0 + jnp.exp(-x))


def _silu(x):
    return x * _sigmoid(x)


def _params(semantics):
    return pltpu.CompilerParams(dimension_semantics=semantics, vmem_limit_bytes=V7X_VMEM_LIMIT)


def _norm_proj_kernel(x_ref, gain_ref, w_ref, qg_ref, kg_ref, cos_ref, sin_ref, bd_ref,
                      hg_ref, kseg_ref, vseg_ref, gate_ref, qt_out, ks_out, vst_out, kw_out, vwt_out,
                      kc_scr, vc_scr):
    x = x_ref[...]
    ms = jnp.mean(x * x, axis=-1, keepdims=True)
    xn = (x * lax.rsqrt(ms + RMS_EPS) * gain_ref[...]).astype(BF16)
    _nsa_prep(_dot(xn, w_ref[:, _HG_END:_NQ_END]), _dot(xn, w_ref[:, _VC_END:_KV_END]),
              qg_ref, kg_ref, cos_ref, sin_ref, bd_ref, qt_out, ks_out, vst_out, kw_out, vwt_out)
    kc_scr[...] = _dot(xn, w_ref[:, _NQ_END:_KC_END])
    vc_scr[...] = _dot(xn, w_ref[:, _KC_END:_VC_END])
    nseg = kseg_ref.shape[0]
    for tok in range(CMP_STRIDE):
        cols = slice(tok * NSA_KV_WIDTH, (tok + 1) * NSA_KV_WIDTH)
        kseg_ref[:, cols] = kc_scr[pl.ds(tok, nseg, stride=CMP_STRIDE), :]
        vseg_ref[:, cols] = vc_scr[pl.ds(tok, nseg, stride=CMP_STRIDE), :]
    gate_ref[...] = _dot(xn, w_ref[:, _KV_END:_GATE_END])
    hg_ref[...] = _dot(xn, w_ref[:, :_HG_END])


def _norm_proj(h, gain, w_pad, q_gain, k_gain, batch, seq):
    n = h.shape[0]
    rows = min(PROJ_ROWS, seq)
    nt = seq // rows
    assert seq // SLC_LEN <= NSA_DIM, "selection one-hot must fit the spare key lanes"
    cos_t, sin_t = _rope_tables(jnp.arange(seq), NSA_HEADS)
    gid = np.arange(NSA_WIDTH) // NSA_DIM
    bd = jnp.asarray((gid[:, None] == gid[None, :]).astype(np.float32), BF16)
    qg = jnp.tile(q_gain, NSA_HEADS)[None, :]
    kg = jnp.stack([jnp.tile(k_gain[1], NSA_KV_HEADS), jnp.tile(k_gain[2], NSA_KV_HEADS)])
    segw = CMP_STRIDE * NSA_KV_WIDTH
    f32_outs = ((rows, n, _HG_END), (rows // CMP_STRIDE, n // CMP_STRIDE, segw),
                (rows // CMP_STRIDE, n // CMP_STRIDE, segw), (rows, n, NSA_KV_HEADS * LANES))
    k_shape =jax.ShapeDtypeStruct((batch, NSA_KV_HEADS, seq, K_AUG), BF16)
    k_spec = pl.BlockSpec((1, NSA_KV_HEADS, rows, K_AUG), lambda b, c: (b, 0, c, 0))
    vt_shape = jax.ShapeDtypeStruct((batch, NSA_KV_HEADS, V_AUG, seq), BF16)
    vt_spec = pl.BlockSpec((1, NSA_KV_HEADS, V_AUG, rows), lambda b, c: (b, 0, 0, c))

    def const(shape):
        return pl.BlockSpec(shape, lambda b, c: (0,) * len(shape))

    return pl.pallas_call(
        _norm_proj_kernel,
        grid=(batch, nt),
        in_specs=[
            pl.BlockSpec((rows, D_MODEL), lambda b, c: (b * nt + c, 0)),
            const((1, D_MODEL)),
            const((D_MODEL, _GATE_END)),
            const((1, NSA_WIDTH)),
            const((2, NSA_KV_WIDTH)),
            pl.BlockSpec((rows, NSA_WIDTH), lambda b, c: (c, 0)),
            pl.BlockSpec((rows, NSA_WIDTH), lambda b, c: (c, 0)),
            const((NSA_WIDTH, NSA_WIDTH)),
        ],
        out_specs=[pl.BlockSpec((r, w), lambda b, c: (b * nt + c, 0)) for r, _, w in f32_outs] + [
            pl.BlockSpec((1, NSA_HEADS, NSA_DIM, rows), lambda b, c: (b, 0, 0, c)),
            k_spec, vt_spec, k_spec, vt_spec,
        ],
        out_shape=[jax.ShapeDtypeStruct((total, w), F32) for _, total, w in f32_outs] + [
            jax.ShapeDtypeStruct((batch, NSA_HEADS, NSA_DIM, seq), BF16),
            k_shape, vt_shape, k_shape, vt_shape,
        ],
        scratch_shapes=[pltpu.VMEM((rows, NSA_KV_WIDTH), F32), pltpu.VMEM((rows, NSA_KV_WIDTH), F32)],
        compiler_params=_params(("parallel", "parallel")),
        name="norm_proj",
    )(h, gain, w_pad, qg, kg, cos_t, sin_t, bd)


def _hgrn_kernel(q_seqs, f_seqs, i_seqs, g_seqs, lbl_ref, gn_ref, tri_ref, o_seqs,
                 s_seqs, b_seqs, k_seqs, *, layer, depth):
    nseq, rows = q_seqs.shape[0], q_seqs.shape[1]
    n_chunks = rows // HG_CHUNK
    c = HG_CHUNK

    @pl.when(pl.program_id(1) == 0)
    def _():
        s_seqs[...] = jnp.zeros_like(s_seqs)

    lg = lbl_ref[...]
    rows_l = [lg[i:i + 1, :] for i in range(depth)]
    mx = functools.reduce(jnp.maximum, rows_l)
    ex = [jnp.exp(r - mx) for r in rows_l]
    den = functools.reduce(lambda a, b_: a + b_, ex)
    pr = [e / den for e in ex]
    lb = functools.reduce(lambda a, b_: a + b_, pr[:layer + 1]) - pr[0]

    def gates(f_ref):
        f = lb + (1.0 - lb) * _sigmoid(f_ref[...])
        w = f.shape[1]
        b3 = _dot(tri_ref[...], jnp.concatenate(_split3(jnp.log(f)), axis=1))
        return 1.0 - f, b3[:, :w] + b3[:, w:2 * w] + b3[:, 2 * w:]

    gated = [gates(f_seqs.at[e]) for e in range(nseq)]
    needs_exact = functools.reduce(jnp.minimum, [jnp.min(b) for _, b in gated]) < -HG_SAFE_DECAY
    gn = gn_ref[...]

    def make_finish(o_ref, g_ref):
        def finish(o, rows_sl, cs):
            ms = jnp.mean(o * o, axis=-1, keepdims=True)
            y = o * lax.rsqrt(ms + RMS_EPS) * gn
            o_ref[rows_sl, cs] = (y * _silu(g_ref[rows_sl, cs])).astype(o_ref.dtype)
        return finish

    def fast_path(q_ref, i_ref, g_ref, o_ref, s_ref, kk, b_all):
        finish = make_finish(o_ref, g_ref)
        nb = n_chunks // 2
        blk = 2 * c
        qf = _silu(q_ref[...])
        qt_l, kt_l, kh_l, qi_l, kb_l, dec = [], [], [], [], [], []
        for j in range(nb):
            parts = []
            for half in range(2):
                rs = slice((2 * j + half) * c, (2 * j + half + 1) * c)
                b = b_all[rs]
                tot = b[c - 1:c, :]
                parts.append((qf[rs] * jnp.exp(b), kk[rs] * jnp.exp(-b), kk[rs] * jnp.exp(tot - b), jnp.exp(tot)))
            (qa, ka, ha, ea), (qb, kb, hb, eb) = parts
            qt_l += [qa, qb]
            kt_l += [ka, kb]
            kh_l += [ha, hb]
            qi_l.append(jnp.concatenate([qa, qb * ea], axis=0))
            kb_l.append(jnp.concatenate([ha * eb, hb], axis=0))
            dec.append(ea * eb)

        def span_decay(lo, hi):
            return functools.reduce(lambda x, y: x * y, dec[lo:hi]) if hi > lo else None

        def scaled(x, d):
            return x if d is None else x * d

        qt = jnp.concatenate(qt_l, axis=0).astype(BF16)
        k_in = jnp.concatenate(kt_l + kh_l, axis=0).astype(BF16)
        kb_all = jnp.concatenate(kb_l, axis=0).astype(BF16)
        q_back = jnp.concatenate(
            [scaled(qi_l[j], span_decay(max(j - dist + 1, 0), j)) for dist in range(1, nb) for j in range(nb)],
            axis=0).astype(BF16)
        q_state = jnp.concatenate([scaled(qi_l[j], span_decay(0, j)) for j in range(nb)], axis=0).astype(BF16)
        k_state = jnp.concatenate([scaled(kb_l[j], span_decay(j + 1, nb)) for j in range(nb)], axis=0).astype(BF16)
        dec_all = span_decay(0, nb)
        t_io = lax.broadcasted_iota(jnp.int32, (rows, rows), 0)
        s_io = lax.broadcasted_iota(jnp.int32, (rows, rows), 1)
        same_chunk = (t_io // c == s_io // c) & (s_io <= t_io)
        cross = (t_io // blk == s_io // blk) & (t_io // c > s_io // c)
        back = [t_io // blk - s_io // blk == dist for dist in range(1, nb)]

        def head(h):
            cs = slice(h * HG_DIM, (h + 1) * HG_DIM)
            v = i_ref[:, cs]
            near = _dot_nt(qt[:, cs], k_in[:, cs])
            a = jnp.where(same_chunk, near[:, :rows], 0.0) + jnp.where(cross, near[:, rows:], 0.0)
            if nb > 1:
                far = _dot_nt(q_back[:, cs], kb_all[:, cs])
                for i, m in enumerate(back):
                    a = a + jnp.where(m, far[i * rows:(i + 1) * rows], 0.0)
            st = s_ref[h]
            o = _dot(a.astype(BF16), v.astype(BF16)) + _dot_nt(q_state[:, cs], st.astype(BF16))
            s_ref[h] = st * dec_all[:, cs] + _dot(v.T.astype(BF16), k_state[:, cs])
            finish(o, slice(None), cs)

        return head

    def exact_path(q_ref, i_ref, g_ref, o_ref, s_ref, b_ref, k_ref, kk, b_all):
        finish = make_finish(o_ref, g_ref)
        b_ref[...] = b_all
        k_ref[...] = kk
        t_io = lax.broadcasted_iota(jnp.int32, (c, c), 0)
        s_io = lax.broadcasted_iota(jnp.int32, (c, c), 1)
        row_io = lax.broadcasted_iota(jnp.int32, (c, HG_DIM), 0)

        def chunk_step(ci, _):
            r0 = pl.multiple_of(ci * c, c)
            for h in range(HG_HEADS):
                cs = slice(h * HG_DIM, (h + 1) * HG_DIM)
                qf = _silu(q_ref[pl.ds(r0, c), cs])
                b = b_ref[pl.ds(r0, c), cs]
                k = k_ref[pl.ds(r0, c), cs]
                v = i_ref[pl.ds(r0, c), cs]
                b_last = b[c - 1:c, :]
                a = jnp.zeros((c, c), F32)
                for s in range(c):
                    d = jnp.where(row_io >= s, b - b[s:s + 1, :], NEG)
                    col = jnp.sum(qf * k[s:s + 1, :] * jnp.exp(d), axis=-1, keepdims=True)
                    a = jnp.where(s_io == s, col, a)
                st = s_ref[h]
                o = _dot(a.astype(BF16), v.astype(BF16)) + _dot_nt((qf * jnp.exp(b)).astype(BF16), st.astype(BF16))
                s_ref[h] = st * jnp.exp(b_last) + _dot(v.T.astype(BF16), (k * jnp.exp(b_last - b)).astype(BF16))
                finish(o, pl.ds(r0, c), cs)

        lax.fori_loop(0, n_chunks, chunk_step, None)

    @pl.when(jnp.logical_not(needs_exact))
    def _():
        heads = [fast_path(q_seqs.at[e], i_seqs.at[e], g_seqs.at[e], o_seqs.at[e], s_seqs.at[e], *gated[e])
                 for e in range(nseq)]
        for h in range(HG_HEADS):
            for e in range(nseq):
                heads[e](h)

    @pl.when(needs_exact)
    def _():
        for e in range(nseq):
            exact_path(q_seqs.at[e], i_seqs.at[e], g_seqs.at[e], o_seqs.at[e], s_seqs.at[e],
                       b_seqs.at[e], k_seqs.at[e], *gated[e])


def _hgrn(hgp, lb_logits, gnorm, layer, batch, seq):
    rows = min(HG_ROWS, seq)
    nt = seq // rows
    nseq = HG_SEQS if batch % HG_SEQS == 0 else 1
    depth = lb_logits.shape[0]
    idx = np.arange(rows)
    tri = ((idx[:, None] >= idx[None, :]) & (idx[:, None] // HG_CHUNK == idx[None, :] // HG_CHUNK))
    tri = jnp.asarray(tri.astype(np.float32), BF16)
    hg3 = hgp.reshape(batch, seq, 4 * HG_WIDTH)

    def col_spec(j):
        return pl.BlockSpec((nseq, rows, HG_WIDTH), lambda b, c, j=j: (b, c, j))

    out = pl.pallas_call(
        functools.partial(_hgrn_kernel, layer=layer, depth=depth),
        grid=(batch // nseq, nt),
        in_specs=[
            col_spec(0), col_spec(1), col_spec(2), col_spec(3),
            pl.BlockSpec((depth, HG_WIDTH), lambda b, c: (0, 0)),
            pl.BlockSpec((1, HG_DIM), lambda b, c: (0, 0)),
            pl.BlockSpec((rows, rows), lambda b, c: (0, 0)),
        ],
        out_specs=pl.BlockSpec((nseq, rows, HG_WIDTH), lambda b, c: (b, c, 0)),
        out_shape=jax.ShapeDtypeStruct((batch, seq, HG_WIDTH), BF16),
        scratch_shapes=[
            pltpu.VMEM((nseq, HG_HEADS, HG_DIM, HG_DIM), F32),
            pltpu.VMEM((nseq, rows, HG_WIDTH), F32),
            pltpu.VMEM((nseq, rows, HG_WIDTH), F32),
        ],
        compiler_params=_params(("parallel", "arbitrary")),
        name="hgrn",
    )(hg3, hg3, hg3, hg3, lb_logits, gnorm, tri)
    return out.reshape(batch * seq, HG_WIDTH)


def _group_rms(x, bd, gain):
    sq = x * x
    hi = sq.astype(BF16)
    lo = (sq - hi.astype(F32)).astype(BF16)
    ss = _dot(hi, bd) + _dot(lo, bd)
    return x * lax.rsqrt(ss * (1.0 / NSA_DIM) + RMS_EPS) * gain


def _rope_lanes(x, cos, sin_signed):
    w = x.shape[-1]
    half = NSA_DIM // 2
    lane = lax.broadcasted_iota(jnp.int32, x.shape, 1)
    first = (lane & (NSA_DIM - 1)) < half
    swapped = jnp.where(first, pltpu.roll(x, w - half, 1), pltpu.roll(x, half, 1))
    return x * cos + swapped * sin_signed


def _nsa_prep(nq, kv, qg_ref, kg_ref, cos_ref, sin_ref, bd_ref, qt_out, ks_out, vst_out, kw_out, vwt_out):
    cos = cos_ref[...]
    sin = sin_ref[...]
    bd = bd_ref[...]
    kvw = NSA_KV_WIDTH
    rows = nq.shape[0]
    q = _rope_lanes(_group_rms(nq, bd, qg_ref[...]), cos, sin) * (NSA_DIM ** -0.5 * LOG2_E)
    qt_out[0] = q.T.reshape(NSA_HEADS, NSA_DIM, rows).astype(BF16)
    bdk = bd[:kvw, :kvw]
    cosk = cos[:, :kvw]
    sink = sin[:, :kvw]
    ks = _rope_lanes(_group_rms(kv[:, 0:kvw], bdk, kg_ref[0:1, :]), cosk, sink)
    kw = _rope_lanes(_group_rms(kv[:, 2 * kvw:3 * kvw], bdk, kg_ref[1:2, :]), cosk, sink)
    pad_rows = V_AUG - NSA_DIM
    ones_row = jnp.where(lax.broadcasted_iota(jnp.int32, (NSA_KV_HEADS, pad_rows, rows), 1) == 0, 1.0, 0.0)

    def values_t(v):
        vt = v.T.reshape(NSA_KV_HEADS, NSA_DIM, rows)
        return jnp.concatenate([vt, ones_row], axis=1).astype(BF16)

    vst_out[0] = values_t(kv[:, kvw:2 * kvw])
    vwt_out[0] = values_t(kv[:, 3 * kvw:4 * kvw])
    tok = pl.program_id(1) * rows + lax.broadcasted_iota(jnp.int32, (rows, NSA_DIM), 0)
    onehot = jnp.where(lax.broadcasted_iota(jnp.int32, (rows, NSA_DIM), 1) == tok // SLC_LEN, 1.0, 0.0)
    zeros = jnp.zeros((rows, NSA_DIM), F32)
    for g in range(NSA_KV_HEADS):
        sl = slice(g * NSA_DIM, (g + 1) * NSA_DIM)
        ks_out[0, g] = jnp.concatenate([ks[:, sl], onehot], axis=1).astype(BF16)
        kw_out[0, g] = jnp.concatenate([kw[:, sl], zeros], axis=1).astype(BF16)


def _rope_tables(pos, reps):
    half = NSA_DIM // 2
    freqs = ROPE_THETA ** (-jnp.arange(half, dtype=F32) / half)
    ang = pos.astype(F32)[:, None] * freqs[None, :]
    cos = jnp.cos(ang)
    sin = jnp.sin(ang)
    cos_t = jnp.tile(jnp.concatenate([cos, cos], axis=1), (1, reps))
    sin_t = jnp.tile(jnp.concatenate([-sin, sin], axis=1), (1, reps))
    return cos_t, sin_t


def _nsa_cmp_kernel(kseg_ref, vseg_ref, pek_ref, pev_ref, w1k_ref, w1v_ref, w2k_ref, w2v_ref,
                    kg_ref, cos_ref, sin_ref, kc_out, vct_out):
    nseg = kseg_ref.shape[1]

    def hidden(seg, pe_ref, w1_ref):
        y0 = _dot((seg + pe_ref[0:1, :]).astype(BF16), w1_ref[0])
        y1 = _dot((seg + pe_ref[1:2, :]).astype(BF16), w1_ref[1])
        act = _silu(y0 + pltpu.roll(y1, nseg - 1, 0)).astype(BF16)
        return [act[:, g * CMP_HIDDEN:(g + 1) * CMP_HIDDEN] for g in range(NSA_KV_HEADS)]

    hk = hidden(kseg_ref[0], pek_ref, w1k_ref)
    hv = hidden(vseg_ref[0], pev_ref, w1v_ref)
    half = NSA_DIM // 2
    for g in range(NSA_KV_HEADS):
        x = _dot(hk[g], w2k_ref[...])
        ms = jnp.mean(x * x, axis=-1, keepdims=True)
        xn = x * lax.rsqrt(ms + RMS_EPS) * kg_ref[...]
        swapped = jnp.concatenate([xn[:, half:], xn[:, :half]], axis=1)
        kc_out[0, g] = (xn * cos_ref[...] + swapped * sin_ref[...]).astype(BF16)
        vct_out[0, g] = _dot_nt(w2v_ref[...], hv[g]).astype(BF16)


def _cmp_weights(pe, w1):
    pe_r = pe.reshape(2, CMP_STRIDE, 1, NSA_DIM)
    pe_seg = jnp.broadcast_to(pe_r, (2, CMP_STRIDE, NSA_KV_HEADS, NSA_DIM)).reshape(2, CMP_STRIDE * NSA_KV_WIDTH)
    w1r = w1.reshape(2, CMP_STRIDE, NSA_DIM, CMP_HIDDEN)
    eye = jnp.eye(NSA_KV_HEADS, dtype=w1.dtype)
    wbig = jnp.einsum('rldj,gh->rlgdhj', w1r, eye)
    wbig = wbig.reshape(2, CMP_STRIDE * NSA_KV_WIDTH, NSA_KV_HEADS * CMP_HIDDEN)
    return pe_seg, wbig.astype(BF16)


def _nsa_cmp(kc_seg, vc_seg, pe_k, w1_k, w2_k, pe_v, w1_v, w2_v, k_gain0, batch, seq):
    nseg = seq // CMP_STRIDE
    segw = CMP_STRIDE * NSA_KV_WIDTH
    kseg = kc_seg.reshape(batch, nseg, segw)
    vseg = vc_seg.reshape(batch, nseg, segw)
    pek, w1k = _cmp_weights(pe_k, w1_k)
    pev, w1v = _cmp_weights(pe_v, w1_v)
    cmp_end = jnp.arange(nseg) * CMP_STRIDE + CMP_LEN - 1
    cos_t, sin_t = _rope_tables(cmp_end, 1)
    hid2 = NSA_KV_HEADS * CMP_HIDDEN
    seg_spec = pl.BlockSpec((1, nseg, segw), lambda b: (b, 0, 0))
    out_spec = pl.BlockSpec((1, NSA_KV_HEADS, nseg, NSA_DIM), lambda b: (b, 0, 0, 0))
    out_shape = jax.ShapeDtypeStruct((batch, NSA_KV_HEADS, nseg, NSA_DIM), BF16)
    out_t_spec = pl.BlockSpec((1, NSA_KV_HEADS, NSA_DIM, nseg), lambda b: (b, 0, 0, 0))
    out_t_shape = jax.ShapeDtypeStruct((batch, NSA_KV_HEADS, NSA_DIM, nseg), BF16)
    return pl.pallas_call(
        _nsa_cmp_kernel,
        grid=(batch,),
        in_specs=[
            seg_spec, seg_spec,
            pl.BlockSpec((2, segw), lambda b: (0, 0)),
            pl.BlockSpec((2, segw), lambda b: (0, 0)),
            pl.BlockSpec((2, segw, hid2), lambda b: (0, 0, 0)),
            pl.BlockSpec((2, segw, hid2), lambda b: (0, 0, 0)),
            pl.BlockSpec((CMP_HIDDEN, NSA_DIM), lambda b: (0, 0)),
            pl.BlockSpec((NSA_DIM, CMP_HIDDEN), lambda b: (0, 0)),
            pl.BlockSpec((1, NSA_DIM), lambda b: (0, 0)),
            pl.BlockSpec((nseg, NSA_DIM), lambda b: (0, 0)),
            pl.BlockSpec((nseg, NSA_DIM), lambda b: (0, 0)),
        ],
        out_specs=[out_spec, out_t_spec],
        out_shape=[out_shape, out_t_shape],
        compiler_params=_params(("parallel",)),
        name="nsa_cmp",
    )(kseg, vseg, pek, pev, w1k, w1v, w2_k.astype(BF16), w2_v.T.astype(BF16),
      k_gain0[None, :], cos_t, sin_t)


def _nsa_attn_kernel(qt_ref, kc_ref, vct_ref, ks_ref, vst_ref, kw_ref, vwt_ref, gate_ref,
                     mcs_ref, bound_ok_ref, o_ref, *, seq, top):
    tq = qt_ref.shape[3]
    tk = min(ATT_K, seq)
    nseg = kc_ref.shape[2]
    nslc = mcs_ref.shape[0]
    t0 = pl.program_id(1) * tq
    groups = range(kc_ref.shape[1])
    qts = [jnp.concatenate([qt_ref[0, g * NSA_GROUP + h] for h in range(NSA_GROUP)], axis=1) for g in groups]

    def tile_heads(x):
        return jnp.concatenate([x] * NSA_GROUP, axis=1)

    blk_n = lax.broadcasted_iota(jnp.int32, (nseg, tq), 0)
    t_col = t0 + lax.broadcasted_iota(jnp.int32, (nseg, tq), 1)
    valid = (blk_n * CMP_STRIDE + (CMP_LEN - 1)) <= t_col
    j_io = lax.broadcasted_iota(jnp.int32, (nslc, tq), 0)
    cur = (t0 + lax.broadcasted_iota(jnp.int32, (nslc, tq), 1)) // SLC_LEN
    forced = (j_io == 0) | (j_io == cur) | (j_io == cur - 1)
    mcs = mcs_ref[...]
    o_cmps = []
    q_aug = []
    for g in groups:
        sc = _dot(kc_ref[0, g], qts[g])
        p_heads = []
        for h in range(NSA_GROUP):
            sm = jnp.where(valid, sc[:, h * tq:(h + 1) * tq], NEG)
            p = jnp.where(valid, jnp.exp2(sm - jnp.max(sm, axis=0, keepdims=True)), 0.0)
            p_heads.append(p / jnp.maximum(jnp.sum(p, axis=0, keepdims=True), 1e-30))
        o_cmps.append(_dot(vct_ref[0, g], jnp.concatenate(p_heads, axis=1).astype(BF16)))

        p_sum = functools.reduce(lambda a, b: a + b, p_heads)
        imp = functools.reduce(lambda a, b: a + b, [_dot(mcs, piece) for piece in _split3(p_sum)])
        score = jnp.where(forced, FORCED_SCORE, jnp.where(j_io <= cur, imp, NEG))
        rank = jnp.zeros((nslc, tq), F32)
        for jp in range(nslc):
            other = score[jp:jp + 1, :]
            ahead = (other > score) | ((other == score) & (j_io > jp))
            rank = rank + jnp.where(ahead, 1.0, 0.0)
        sel_bias = jnp.where(rank < top, 0.0, NEG)
        sel_rows = jnp.concatenate([sel_bias, jnp.zeros((K_AUG - NSA_DIM - nslc, tq), F32)], axis=0)
        q_aug.append(jnp.concatenate([qts[g], tile_heads(sel_rows).astype(BF16)], axis=0))

    lanes = NSA_GROUP * tq
    init1 = (jnp.full((1, lanes), NEG, F32), jnp.zeros((V_AUG, lanes), F32))
    init = tuple((init1, init1) for _ in groups)
    k_io = lax.broadcasted_iota(jnp.int32, (tk, tq), 0)
    t_io = t0 + lax.broadcasted_iota(jnp.int32, (tk, tq), 1)
    diag = t0 // tk
    n_win = -(-(WINDOW - 1) // tk) + 1

    def online_step(carry, s, vt):
        m_old, acc = carry
        m_new = jnp.maximum(m_old, jnp.max(s, axis=0, keepdims=True))
        pe = jnp.exp2(s - m_new).astype(BF16)
        return m_new, jnp.exp2(m_old - m_new) * acc + _dot(vt, pe)

    def bounded_step(carry, s, vt):
        return carry[0], carry[1] + _dot(vt, jnp.exp2(s).astype(BF16))

    def tile_step(kt, carries, window, causal=False, tail=True, span=1, online_step=online_step):
        k0 = pl.multiple_of(kt * tk, tk)
        kpos = k0 + k_io
        out = []
        for g in groups:
            c_slc, c_swa = carries[g]
            if window:
                keys = jnp.concatenate([ks_ref[0, g, pl.ds(k0, tk), :], kw_ref[0, g, pl.ds(k0, tk), :]], axis=0)
                s2 = _dot(keys, q_aug[g])
                s_slc, s_swa = s2[:tk], s2[tk:]
                if causal:
                    s_slc = s_slc + tile_heads(jnp.where(kpos <= t_io, 0.0, NEG))
                if causal and tail:
                    s_swa = s_swa + tile_heads(jnp.where((kpos <= t_io) & (kpos > t_io - WINDOW), 0.0, NEG))
                elif causal:
                    s_swa = s_swa + tile_heads(jnp.where(kpos <= t_io, 0.0, NEG))
                elif tail:
                    s_swa = s_swa + tile_heads(jnp.where(kpos > t_io - WINDOW, 0.0, NEG))
                c_slc = online_step(c_slc, s_slc, vst_ref[0, g, :, pl.ds(k0, tk)])
                c_swa = online_step(c_swa, s_swa, vwt_ref[0, g, :, pl.ds(k0, tk)])
            else:
                s = _dot(ks_ref[0, g, pl.ds(k0, span * tk), :], q_aug[g])
                c_slc = online_step(c_slc, s, vst_ref[0, g, :, pl.ds(k0, span * tk)])
            out.append((c_slc, c_swa))
        return tuple(out)

    rest = jnp.maximum(diag - (n_win - 1), 0)

    def attend(step):
        carries = tile_step(diag, init, True, causal=True, tail=tk - 1 >= WINDOW, online_step=step)
        carries = lax.fori_loop(
            1, jnp.minimum(n_win, diag + 1), lambda i, c: tile_step(diag - i, c, True, online_step=step), carries)
        carries = lax.fori_loop(
            0, rest // 2, lambda p, c: tile_step(rest - 2 - 2 * p, c, False, span=2, online_step=step), carries)
        carries = lax.cond(rest % 2 == 1, lambda c: tile_step(0, c, False, online_step=step), lambda c: c, carries)

        grp_w = NSA_GROUP * NSA_DIM
        for g in groups:
            (_, acc_s), (_, acc_w) = carries[g]
            o_slc = acc_s[:NSA_DIM] / acc_s[NSA_DIM:NSA_DIM + 1]
            o_swa = acc_w[:NSA_DIM] / acc_w[NSA_DIM:NSA_DIM + 1]
            e, kvg = divmod(g, NSA_KV_HEADS)
            gate = _sigmoid(gate_ref[0, e, :, kvg * LANES:(kvg + 1) * LANES].T)
            outs = []
            for h in range(NSA_GROUP):
                ls = slice(h * tq, (h + 1) * tq)
                g0 = gate[N_BRANCH * h:N_BRANCH * h + 1, :]
                g1 = gate[N_BRANCH * h + 1:N_BRANCH * h + 2, :]
                g2 = gate[N_BRANCH * h + 2:N_BRANCH * h + 3, :]
                outs.append(g0 * o_cmps[g][:, ls] + g1 * o_slc[:, ls] + g2 * o_swa[:, ls])
            o_ref[0, e, :, kvg * grp_w:(kvg + 1) * grp_w] = jnp.concatenate(outs, axis=0).T.astype(o_ref.dtype)

    bounded = bound_ok_ref[0] != 0

    @pl.when(bounded)
    def _():
        attend(bounded_step)

    @pl.when(jnp.logical_not(bounded))
    def _():
        attend(online_step)


def _score_bound_ok(q_gain, k_gain):
    k_max = jnp.max(jnp.abs(k_gain[1:]))
    bound = NSA_DIM * jnp.max(jnp.abs(q_gain)) * k_max * (NSA_DIM ** -0.5 * LOG2_E)
    return (bound <= ATT_SAFE_SCORE).astype(jnp.int32).reshape(1)


def _nsa_attn(qt, kc, vct, ks, vst, kw, vwt, gate, bound_ok, batch, seq):
    tq = min(ATT_Q, seq)
    nq = seq // tq
    nseg = seq // CMP_STRIDE
    nslc = seq // SLC_LEN
    top = min(SLC_TOP, nslc)
    c_start = np.arange(nseg) * CMP_STRIDE
    c_end = c_start + CMP_LEN - 1
    s_start = np.arange(nslc) * SLC_LEN
    s_end = s_start + SLC_LEN - 1
    overlap = (c_start[None, :] <= s_end[:, None]) & (c_end[None, :] >= s_start[:, None])
    overlap[:, nseg - 1] = False
    mcs = jnp.asarray(overlap.astype(np.float32), BF16)
    ns = ATT_SEQS if batch % ATT_SEQS == 0 else 1
    steps = batch // ns

    def fold(x):
        return x.reshape((steps, ns * x.shape[1]) + x.shape[2:])

    def seq_block(shape):
        return pl.BlockSpec((1,) + shape, lambda b, i: (b, 0, 0, 0))

    k_full = seq_block((ns * NSA_KV_HEADS, seq, K_AUG))
    vt_full = seq_block((ns * NSA_KV_HEADS, V_AUG, seq))
    out = pl.pallas_call(
        functools.partial(_nsa_attn_kernel, seq=seq, top=top),
        grid=(steps, nq),
        in_specs=[
            pl.BlockSpec((1, ns * NSA_HEADS, NSA_DIM, tq), lambda b, i: (b, 0, 0, i)),
            seq_block((ns * NSA_KV_HEADS, nseg, NSA_DIM)),
            seq_block((ns * NSA_KV_HEADS, NSA_DIM, nseg)),
            k_full, vt_full, k_full, vt_full,
            pl.BlockSpec((1, ns, tq, NSA_KV_HEADS * LANES), lambda b, i: (b, 0, i, 0)),
            pl.BlockSpec((nslc, nseg), lambda b, i: (0, 0)),
            pl.BlockSpec(memory_space=pltpu.SMEM),
        ],
        out_specs=pl.BlockSpec((1, ns, tq, NSA_WIDTH), lambda b, i: (b, 0, i, 0)),
        out_shape=jax.ShapeDtypeStruct((steps, ns, seq, NSA_WIDTH), BF16),
        compiler_params=_params(("parallel", "arbitrary")),
        name="nsa_attn",
    )(fold(qt), fold(kc), fold(vct), fold(ks), fold(vst), fold(kw), fold(vwt),
      gate.reshape(steps, ns, seq, NSA_KV_HEADS * LANES), mcs, bound_ok)
    return out.reshape(batch * seq, NSA_WIDTH)


def _out_ffn_kernel(h_ref, ohg_ref, onsa_ref, wout_ref, gain_ref, wg_ref, wu_ref, wo_ref,
                    o_ref, xn_ref, acc_ref):
    j = pl.program_id(1)

    @pl.when(j == 0)
    def _():
        h1 = (h_ref[...] + _dot(ohg_ref[...], wout_ref[:HG_WIDTH, :])
              + _dot(onsa_ref[...], wout_ref[HG_WIDTH:, :]))
        acc_ref[...] = h1
        ms = jnp.mean(h1 * h1, axis=-1, keepdims=True)
        xn_ref[...] = (h1 * lax.rsqrt(ms + RMS_EPS) * gain_ref[...]).astype(BF16)

    xn = xn_ref[...]
    act = (_silu(_dot(xn, wg_ref[...])) * _dot(xn, wu_ref[...])).astype(BF16)
    acc_ref[...] += _dot(act, wo_ref[...])

    @pl.when(j == pl.num_programs(1) - 1)
    def _():
        o_ref[...] = acc_ref[...]


def _out_ffn(h, o_hg, o_nsa, w_out, gain, w_ffn_in, w_ffn_out):
    n = h.shape[0]
    tm = PROJ_ROWS
    nff = D_FF // FF_TILE
    mixw = HG_WIDTH + NSA_WIDTH
    return pl.pallas_call(
        _out_ffn_kernel,
        grid=(n // tm, nff),
        in_specs=[
            pl.BlockSpec((tm, D_MODEL), lambda i, j: (i, 0)),
            pl.BlockSpec((tm, HG_WIDTH), lambda i, j: (i, 0)),
            pl.BlockSpec((tm, NSA_WIDTH), lambda i, j: (i, 0)),
            pl.BlockSpec((mixw, D_MODEL), lambda i, j: (0, 0)),
            pl.BlockSpec((1, D_MODEL), lambda i, j: (0, 0)),
            pl.BlockSpec((D_MODEL, FF_TILE), lambda i, j: (0, j)),
            pl.BlockSpec((D_MODEL, FF_TILE), lambda i, j: (0, nff + j)),
            pl.BlockSpec((FF_TILE, D_MODEL), lambda i, j: (j, 0)),
        ],
        out_specs=pl.BlockSpec((tm, D_MODEL), lambda i, j: (i, 0)),
        out_shape=jax.ShapeDtypeStruct((n, D_MODEL), F32),
        scratch_shapes=[pltpu.VMEM((tm, D_MODEL), BF16), pltpu.VMEM((tm, D_MODEL), F32)],
        compiler_params=_params(("parallel", "arbitrary")),
        name="out_ffn",
    )(h, o_hg, o_nsa, w_out, gain, w_ffn_in, w_ffn_in, w_ffn_out)


def _pad_w_in(w):
    pieces = [w[:, :_KV_END]]
    for g in range(NSA_KV_HEADS):
        cols = w[:, _KV_END + g * GATES_PER_GROUP:_KV_END + (g + 1) * GATES_PER_GROUP]
        pieces.append(jnp.pad(cols, ((0, 0), (0, LANES - GATES_PER_GROUP))))
    return jnp.concatenate(pieces, axis=1).astype(BF16)


def kernel(x, w_in, w_out, hg_lb_logits, hg_gnorm, q_gain, k_gain, cmp_pe_k, cmp_w1_k, cmp_w2_k,
           cmp_pe_v, cmp_w1_v, cmp_w2_v, w_ffn_in, w_ffn_out, norm_mix, norm_ffn):
    batch, seq, _ = x.shape
    depth = w_in.shape[0]
    h = x.reshape(batch * seq, D_MODEL)
    for l in range(depth):
        hgp, kc_tok, vc_tok, gate, qt, ks, vst, kw, vwt = _norm_proj(
            h, norm_mix[l][None, :], _pad_w_in(w_in[l]), q_gain[l], k_gain[l], batch, seq)
        o_hg = _hgrn(hgp, hg_lb_logits, hg_gnorm[l][None, :], l, batch, seq)
        kc, vct = _nsa_cmp(kc_tok, vc_tok, cmp_pe_k[l], cmp_w1_k[l], cmp_w2_k[l],
                           cmp_pe_v[l], cmp_w1_v[l], cmp_w2_v[l], k_gain[l, 0], batch, seq)
        o_nsa = _nsa_attn(qt, kc, vct, ks, vst, kw, vwt, gate, _score_bound_ok(q_gain[l], k_gain[l]), batch, seq)
        h = _out_ffn(h, o_hg, o_nsa, w_out[l].astype(BF16), norm_ffn[l][None, :],
                     w_ffn_in[l].astype(BF16), w_ffn_out[l].astype(BF16))
    return h.reshape(batch, seq, D_MODEL)
```

```python
import functools

import jax
import jax.numpy as jnp
import numpy as np
from jax import lax
from jax.experimental import pallas as pl
from jax.experimental.pallas import tpu as pltpu

F32 = jnp.float32
BF16 = jnp.bfloat16

D_MODEL = 1024
HG_HEADS = 4
HG_DIM = 128
HG_WIDTH = HG_HEADS * HG_DIM
NSA_HEADS = 8
NSA_KV_HEADS = 2
NSA_GROUP = NSA_HEADS // NSA_KV_HEADS
NSA_DIM = 64
NSA_WIDTH = NSA_HEADS * NSA_DIM
NSA_KV_WIDTH = NSA_KV_HEADS * NSA_DIM
CMP_LEN = 32
CMP_STRIDE = 16
CMP_HIDDEN = 256
SLC_LEN = 64
SLC_TOP = 16
WINDOW = 512
N_BRANCH = 3
D_FF = 2816
ROPE_THETA = 10000.0
RMS_EPS = 1e-6
NEG = -1e30
FORCED_SCORE = 1e9
LOG2_E = 1.4426950408889634

LANES = 128
V7X_VMEM_LIMIT = 56 * 1024 * 1024

PROJ_ROWS = 512
HG_ROWS = 128
HG_SEQS = 8
HG_CHUNK = 32
HG_SAFE_DECAY = 60.0
ATT_Q = 256
ATT_K = 256
ATT_SEQS = 2
ATT_SAFE_SCORE = 100.0
FF_TILE = 1408
K_AUG = 2 * NSA_DIM
V_AUG = NSA_DIM + 16

_HG_END = 4 * HG_WIDTH
_NQ_END = _HG_END + NSA_WIDTH
_KC_END = _NQ_END + NSA_KV_WIDTH
_VC_END = _KC_END + NSA_KV_WIDTH
_KV_END = _VC_END + 4 * NSA_KV_WIDTH
_GATE_END = _KV_END + NSA_KV_HEADS * LANES
GATES_PER_GROUP = NSA_GROUP * N_BRANCH


def _dot(a, b):
    return jnp.dot(a, b, preferred_element_type=F32)


def _dot_nt(a, b):
    return lax.dot_general(a, b, (((1,), (1,)), ((), ())), preferred_element_type=F32)


def _split3(x):
    hi = x.astype(BF16)
    r1 = x - hi.astype(F32)
    mid = r1.astype(BF16)
    lo = (r1 - mid.astype(F32)).astype(BF16)
    return hi, mid, lo


def _sigmoid(x):
    return 1.0 / (1.0 + jnp.exp(-x))


def _silu(x):
    return x * _sigmoid(x)


def _params(semantics):
    return pltpu.CompilerParams(dimension_semantics=semantics, vmem_limit_bytes=V7X_VMEM_LIMIT)


def _norm_proj_kernel(x_ref, gain_ref, w_ref, qg_ref, kg_ref, cos_ref, sin_ref, bd_ref,
                      hg_ref, kseg_ref, vseg_ref, gate_ref, qt_out, ks_out, vst_out, kw_out, vwt_out,
                      kc_scr, vc_scr):
    x = x_ref[...]
    ms = jnp.mean(x * x, axis=-1, keepdims=True)
    xn = (x * lax.rsqrt(ms + RMS_EPS) * gain_ref[...]).astype(BF16)
    _nsa_prep(_dot(xn, w_ref[:, _HG_END:_NQ_END]), _dot(xn, w_ref[:, _VC_END:_KV_END]),
              qg_ref, kg_ref, cos_ref, sin_ref, bd_ref, qt_out, ks_out, vst_out, kw_out, vwt_out)
    kc_scr[...] = _dot(xn, w_ref[:, _NQ_END:_KC_END])
    vc_scr[...] = _dot(xn, w_ref[:, _KC_END:_VC_END])
    nseg = kseg_ref.shape[0]
    for tok in range(CMP_STRIDE):
        cols = slice(tok * NSA_KV_WIDTH, (tok + 1) * NSA_KV_WIDTH)
        kseg_ref[:, cols] = kc_scr[pl.ds(tok, nseg, stride=CMP_STRIDE), :]
        vseg_ref[:, cols] = vc_scr[pl.ds(tok, nseg, stride=CMP_STRIDE), :]
    gate_ref[...] = _dot(xn, w_ref[:, _KV_END:_GATE_END])
    hg_ref[...] = _dot(xn, w_ref[:, :_HG_END])


def _norm_proj(h, gain, w_pad, q_gain, k_gain, batch, seq):
    n = h.shape[0]
    rows = min(PROJ_ROWS, seq)
    nt = seq // rows
    assert seq // SLC_LEN <= NSA_DIM, "selection one-hot must fit the spare key lanes"
    cos_t, sin_t = _rope_tables(jnp.arange(seq), NSA_HEADS)
    gid = np.arange(NSA_WIDTH) // NSA_DIM
    bd = jnp.asarray((gid[:, None] == gid[None, :]).astype(np.float32), BF16)
    qg = jnp.tile(q_gain, NSA_HEADS)[None, :]
    kg = jnp.stack([jnp.tile(k_gain[1], NSA_KV_HEADS), jnp.tile(k_gain[2], NSA_KV_HEADS)])
    segw = CMP_STRIDE * NSA_KV_WIDTH
    f32_outs = ((rows, n, _HG_END), (rows // CMP_STRIDE, n // CMP_STRIDE, segw),
                (rows // CMP_STRIDE, n // CMP_STRIDE, segw), (rows, n, NSA_KV_HEADS * LANES))
    k_shape =jax.ShapeDtypeStruct((batch, NSA_KV_HEADS, seq, K_AUG), BF16)
    k_spec = pl.BlockSpec((1, NSA_KV_HEADS, rows, K_AUG), lambda b, c: (b, 0, c, 0))
    vt_shape = jax.ShapeDtypeStruct((batch, NSA_KV_HEADS, V_AUG, seq), BF16)
    vt_spec = pl.BlockSpec((1, NSA_KV_HEADS, V_AUG, rows), lambda b, c: (b, 0, 0, c))

    def const(shape):
        return pl.BlockSpec(shape, lambda b, c: (0,) * len(shape))

    return pl.pallas_call(
        _norm_proj_kernel,
        grid=(batch, nt),
        in_specs=[
            pl.BlockSpec((rows, D_MODEL), lambda b, c: (b * nt + c, 0)),
            const((1, D_MODEL)),
            const((D_MODEL, _GATE_END)),
            const((1, NSA_WIDTH)),
            const((2, NSA_KV_WIDTH)),
            pl.BlockSpec((rows, NSA_WIDTH), lambda b, c: (c, 0)),
            pl.BlockSpec((rows, NSA_WIDTH), lambda b, c: (c, 0)),
            const((NSA_WIDTH, NSA_WIDTH)),
        ],
        out_specs=[pl.BlockSpec((r, w), lambda b, c: (b * nt + c, 0)) for r, _, w in f32_outs] + [
            pl.BlockSpec((1, NSA_HEADS, NSA_DIM, rows), lambda b, c: (b, 0, 0, c)),
            k_spec, vt_spec, k_spec, vt_spec,
        ],
        out_shape=[jax.ShapeDtypeStruct((total, w), F32) for _, total, w in f32_outs] + [
            jax.ShapeDtypeStruct((batch, NSA_HEADS, NSA_DIM, seq), BF16),
            k_shape, vt_shape, k_shape, vt_shape,
        ],
        scratch_shapes=[pltpu.VMEM((rows, NSA_KV_WIDTH), F32), pltpu.VMEM((rows, NSA_KV_WIDTH), F32)],
        compiler_params=_params(("parallel", "parallel")),
        name="norm_proj",
    )(h, gain, w_pad, qg, kg, cos_t, sin_t, bd)


def _hgrn_kernel(q_seqs, f_seqs, i_seqs, g_seqs, lbl_ref, gn_ref, tri_ref, o_seqs,
                 s_seqs, b_seqs, k_seqs, *, layer, depth):
    nseq, rows = q_seqs.shape[0], q_seqs.shape[1]
    n_chunks = rows // HG_CHUNK
    c = HG_CHUNK

    @pl.when(pl.program_id(1) == 0)
    def _():
        s_seqs[...] = jnp.zeros_like(s_seqs)

    lg = lbl_ref[...]
    rows_l = [lg[i:i + 1, :] for i in range(depth)]
    mx = functools.reduce(jnp.maximum, rows_l)
    ex = [jnp.exp(r - mx) for r in rows_l]
    den = functools.reduce(lambda a, b_: a + b_, ex)
    pr = [e / den for e in ex]
    lb = functools.reduce(lambda a, b_: a + b_, pr[:layer + 1]) - pr[0]

    def gates(f_ref):
        f = lb + (1.0 - lb) * _sigmoid(f_ref[...])
        w = f.shape[1]
        b3 = _dot(tri_ref[...], jnp.concatenate(_split3(jnp.log(f)), axis=1))
        return 1.0 - f, b3[:, :w] + b3[:, w:2 * w] + b3[:, 2 * w:]

    gated = [gates(f_seqs.at[e]) for e in range(nseq)]
    needs_exact = functools.reduce(jnp.minimum, [jnp.min(b) for _, b in gated]) < -HG_SAFE_DECAY
    gn = gn_ref[...]

    def make_finish(o_ref, g_ref):
        def finish(o, rows_sl, cs):
            ms = jnp.mean(o * o, axis=-1, keepdims=True)
            y = o * lax.rsqrt(ms + RMS_EPS) * gn
            o_ref[rows_sl, cs] = (y * _silu(g_ref[rows_sl, cs])).astype(o_ref.dtype)
        return finish

    def fast_path(q_ref, i_ref, g_ref, o_ref, s_ref, kk, b_all):
        finish = make_finish(o_ref, g_ref)
        nb = n_chunks // 2
        blk = 2 * c
        qf = _silu(q_ref[...])
        qt_l, kt_l, kh_l, qi_l, kb_l, dec = [], [], [], [], [], []
        for j in range(nb):
            parts = []
            for half in range(2):
                rs = slice((2 * j + half) * c, (2 * j + half + 1) * c)
                b = b_all[rs]
                tot = b[c - 1:c, :]
                parts.append((qf[rs] * jnp.exp(b), kk[rs] * jnp.exp(-b), kk[rs] * jnp.exp(tot - b), jnp.exp(tot)))
            (qa, ka, ha, ea), (qb, kb, hb, eb) = parts
            qt_l += [qa, qb]
            kt_l += [ka, kb]
            kh_l += [ha, hb]
            qi_l.append(jnp.concatenate([qa, qb * ea], axis=0))
            kb_l.append(jnp.concatenate([ha * eb, hb], axis=0))
            dec.append(ea * eb)

        def span_decay(lo, hi):
            return functools.reduce(lambda x, y: x * y, dec[lo:hi]) if hi > lo else None

        def scaled(x, d):
            return x if d is None else x * d

        qt = jnp.concatenate(qt_l, axis=0).astype(BF16)
        k_in = jnp.concatenate(kt_l + kh_l, axis=0).astype(BF16)
        kb_all = jnp.concatenate(kb_l, axis=0).astype(BF16)
        q_back = jnp.concatenate(
            [scaled(qi_l[j], span_decay(max(j - dist + 1, 0), j)) for dist in range(1, nb) for j in range(nb)],
            axis=0).astype(BF16)
        q_state = jnp.concatenate([scaled(qi_l[j], span_decay(0, j)) for j in range(nb)], axis=0).astype(BF16)
        k_state = jnp.concatenate([scaled(kb_l[j], span_decay(j + 1, nb)) for j in range(nb)], axis=0).astype(BF16)
        dec_all = span_decay(0, nb)
        t_io = lax.broadcasted_iota(jnp.int32, (rows, rows), 0)
        s_io = lax.broadcasted_iota(jnp.int32, (rows, rows), 1)
        same_chunk = (t_io // c == s_io // c) & (s_io <= t_io)
        cross = (t_io // blk == s_io // blk) & (t_io // c > s_io // c)
        back = [t_io // blk - s_io // blk == dist for dist in range(1, nb)]

        def head(h):
            cs = slice(h * HG_DIM, (h + 1) * HG_DIM)
            v = i_ref[:, cs]
            near = _dot_nt(qt[:, cs], k_in[:, cs])
            a = jnp.where(same_chunk, near[:, :rows], 0.0) + jnp.where(cross, near[:, rows:], 0.0)
            if nb > 1:
                far = _dot_nt(q_back[:, cs], kb_all[:, cs])
                for i, m in enumerate(back):
                    a = a + jnp.where(m, far[i * rows:(i + 1) * rows], 0.0)
            st = s_ref[h]
            o = _dot(a.astype(BF16), v.astype(BF16)) + _dot_nt(q_state[:, cs], st.astype(BF16))
            s_ref[h] = st * dec_all[:, cs] + _dot(v.T.astype(BF16), k_state[:, cs])
            finish(o, slice(None), cs)

        return head

    def exact_path(q_ref, i_ref, g_ref, o_ref, s_ref, b_ref, k_ref, kk, b_all):
        finish = make_finish(o_ref, g_ref)
        b_ref[...] = b_all
        k_ref[...] = kk
        t_io = lax.broadcasted_iota(jnp.int32, (c, c), 0)
        s_io = lax.broadcasted_iota(jnp.int32, (c, c), 1)
        row_io = lax.broadcasted_iota(jnp.int32, (c, HG_DIM), 0)

        def chunk_step(ci, _):
            r0 = pl.multiple_of(ci * c, c)
            for h in range(HG_HEADS):
                cs = slice(h * HG_DIM, (h + 1) * HG_DIM)
                qf = _silu(q_ref[pl.ds(r0, c), cs])
                b = b_ref[pl.ds(r0, c), cs]
                k = k_ref[pl.ds(r0, c), cs]
                v = i_ref[pl.ds(r0, c), cs]
                b_last = b[c - 1:c, :]
                a = jnp.zeros((c, c), F32)
                for s in range(c):
                    d = jnp.where(row_io >= s, b - b[s:s + 1, :], NEG)
                    col = jnp.sum(qf * k[s:s + 1, :] * jnp.exp(d), axis=-1, keepdims=True)
                    a = jnp.where(s_io == s, col, a)
                st = s_ref[h]
                o = _dot(a.astype(BF16), v.astype(BF16)) + _dot_nt((qf * jnp.exp(b)).astype(BF16), st.astype(BF16))
                s_ref[h] = st * jnp.exp(b_last) + _dot(v.T.astype(BF16), (k * jnp.exp(b_last - b)).astype(BF16))
                finish(o, pl.ds(r0, c), cs)

        lax.fori_loop(0, n_chunks, chunk_step, None)

    @pl.when(jnp.logical_not(needs_exact))
    def _():
        heads = [fast_path(q_seqs.at[e], i_seqs.at[e], g_seqs.at[e], o_seqs.at[e], s_seqs.at[e], *gated[e])
                 for e in range(nseq)]
        for h in range(HG_HEADS):
            for e in range(nseq):
                heads[e](h)

    @pl.when(needs_exact)
    def _():
        for e in range(nseq):
            exact_path(q_seqs.at[e], i_seqs.at[e], g_seqs.at[e], o_seqs.at[e], s_seqs.at[e],
                       b_seqs.at[e], k_seqs.at[e], *gated[e])


def _hgrn(hgp, lb_logits, gnorm, layer, batch, seq):
    rows = min(HG_ROWS, seq)
    nt = seq // rows
    nseq = HG_SEQS if batch % HG_SEQS == 0 else 1
    depth = lb_logits.shape[0]
    idx = np.arange(rows)
    tri = ((idx[:, None] >= idx[None, :]) & (idx[:, None] // HG_CHUNK == idx[None, :] // HG_CHUNK))
    tri = jnp.asarray(tri.astype(np.float32), BF16)
    hg3 = hgp.reshape(batch, seq, 4 * HG_WIDTH)

    def col_spec(j):
        return pl.BlockSpec((nseq, rows, HG_WIDTH), lambda b, c, j=j: (b, c, j))

    out = pl.pallas_call(
        functools.partial(_hgrn_kernel, layer=layer, depth=depth),
        grid=(batch // nseq, nt),
        in_specs=[
            col_spec(0), col_spec(1), col_spec(2), col_spec(3),
            pl.BlockSpec((depth, HG_WIDTH), lambda b, c: (0, 0)),
            pl.BlockSpec((1, HG_DIM), lambda b, c: (0, 0)),
            pl.BlockSpec((rows, rows), lambda b, c: (0, 0)),
        ],
        out_specs=pl.BlockSpec((nseq, rows, HG_WIDTH), lambda b, c: (b, c, 0)),
        out_shape=jax.ShapeDtypeStruct((batch, seq, HG_WIDTH), BF16),
        scratch_shapes=[
            pltpu.VMEM((nseq, HG_HEADS, HG_DIM, HG_DIM), F32),
            pltpu.VMEM((nseq, rows, HG_WIDTH), F32),
            pltpu.VMEM((nseq, rows, HG_WIDTH), F32),
        ],
        compiler_params=_params(("parallel", "arbitrary")),
        name="hgrn",
    )(hg3, hg3, hg3, hg3, lb_logits, gnorm, tri)
    return out.reshape(batch * seq, HG_WIDTH)


def _group_rms(x, bd, gain):
    sq = x * x
    hi = sq.astype(BF16)
    lo = (sq - hi.astype(F32)).astype(BF16)
    ss = _dot(hi, bd) + _dot(lo, bd)
    return x * lax.rsqrt(ss * (1.0 / NSA_DIM) + RMS_EPS) * gain


def _rope_lanes(x, cos, sin_signed):
    w = x.shape[-1]
    half = NSA_DIM // 2
    lane = lax.broadcasted_iota(jnp.int32, x.shape, 1)
    first = (lane & (NSA_DIM - 1)) < half
    swapped = jnp.where(first, pltpu.roll(x, w - half, 1), pltpu.roll(x, half, 1))
    return x * cos + swapped * sin_signed


def _nsa_prep(nq, kv, qg_ref, kg_ref, cos_ref, sin_ref, bd_ref, qt_out, ks_out, vst_out, kw_out, vwt_out):
    cos = cos_ref[...]
    sin = sin_ref[...]
    bd = bd_ref[...]
    kvw = NSA_KV_WIDTH
    rows = nq.shape[0]
    q = _rope_lanes(_group_rms(nq, bd, qg_ref[...]), cos, sin) * (NSA_DIM ** -0.5 * LOG2_E)
    qt_out[0] = q.T.reshape(NSA_HEADS, NSA_DIM, rows).astype(BF16)
    bdk = bd[:kvw, :kvw]
    cosk = cos[:, :kvw]
    sink = sin[:, :kvw]
    ks = _rope_lanes(_group_rms(kv[:, 0:kvw], bdk, kg_ref[0:1, :]), cosk, sink)
    kw = _rope_lanes(_group_rms(kv[:, 2 * kvw:3 * kvw], bdk, kg_ref[1:2, :]), cosk, sink)
    pad_rows = V_AUG - NSA_DIM
    ones_row = jnp.where(lax.broadcasted_iota(jnp.int32, (NSA_KV_HEADS, pad_rows, rows), 1) == 0, 1.0, 0.0)

    def values_t(v):
        vt = v.T.reshape(NSA_KV_HEADS, NSA_DIM, rows)
        return jnp.concatenate([vt, ones_row], axis=1).astype(BF16)

    vst_out[0] = values_t(kv[:, kvw:2 * kvw])
    vwt_out[0] = values_t(kv[:, 3 * kvw:4 * kvw])
    tok = pl.program_id(1) * rows + lax.broadcasted_iota(jnp.int32, (rows, NSA_DIM), 0)
    onehot = jnp.where(lax.broadcasted_iota(jnp.int32, (rows, NSA_DIM), 1) == tok // SLC_LEN, 1.0, 0.0)
    zeros = jnp.zeros((rows, NSA_DIM), F32)
    for g in range(NSA_KV_HEADS):
        sl = slice(g * NSA_DIM, (g + 1) * NSA_DIM)
        ks_out[0, g] = jnp.concatenate([ks[:, sl], onehot], axis=1).astype(BF16)
        kw_out[0, g] = jnp.concatenate([kw[:, sl], zeros], axis=1).astype(BF16)


def _rope_tables(pos, reps):
    half = NSA_DIM // 2
    freqs = ROPE_THETA ** (-jnp.arange(half, dtype=F32) / half)
    ang = pos.astype(F32)[:, None] * freqs[None, :]
    cos = jnp.cos(ang)
    sin = jnp.sin(ang)
    cos_t = jnp.tile(jnp.concatenate([cos, cos], axis=1), (1, reps))
    sin_t = jnp.tile(jnp.concatenate([-sin, sin], axis=1), (1, reps))
    return cos_t, sin_t


def _nsa_cmp_kernel(kseg_ref, vseg_ref, pek_ref, pev_ref, w1k_ref, w1v_ref, w2k_ref, w2v_ref,
                    kg_ref, cos_ref, sin_ref, kc_out, vct_out):
    nseg = kseg_ref.shape[1]

    def hidden(seg, pe_ref, w1_ref):
        y0 = _dot((seg + pe_ref[0:1, :]).astype(BF16), w1_ref[0])
        y1 = _dot((seg + pe_ref[1:2, :]).astype(BF16), w1_ref[1])
        act = _silu(y0 + pltpu.roll(y1, nseg - 1, 0)).astype(BF16)
        return [act[:, g * CMP_HIDDEN:(g + 1) * CMP_HIDDEN] for g in range(NSA_KV_HEADS)]

    hk = hidden(kseg_ref[0], pek_ref, w1k_ref)
    hv = hidden(vseg_ref[0], pev_ref, w1v_ref)
    half = NSA_DIM // 2
    for g in range(NSA_KV_HEADS):
        x = _dot(hk[g], w2k_ref[...])
        ms = jnp.mean(x * x, axis=-1, keepdims=True)
        xn = x * lax.rsqrt(ms + RMS_EPS) * kg_ref[...]
        swapped = jnp.concatenate([xn[:, half:], xn[:, :half]], axis=1)
        kc_out[0, g] = (xn * cos_ref[...] + swapped * sin_ref[...]).astype(BF16)
        vct_out[0, g] = _dot_nt(w2v_ref[...], hv[g]).astype(BF16)


def _cmp_weights(pe, w1):
    pe_r = pe.reshape(2, CMP_STRIDE, 1, NSA_DIM)
    pe_seg = jnp.broadcast_to(pe_r, (2, CMP_STRIDE, NSA_KV_HEADS, NSA_DIM)).reshape(2, CMP_STRIDE * NSA_KV_WIDTH)
    w1r = w1.reshape(2, CMP_STRIDE, NSA_DIM, CMP_HIDDEN)
    eye = jnp.eye(NSA_KV_HEADS, dtype=w1.dtype)
    wbig = jnp.einsum('rldj,gh->rlgdhj', w1r, eye)
    wbig = wbig.reshape(2, CMP_STRIDE * NSA_KV_WIDTH, NSA_KV_HEADS * CMP_HIDDEN)
    return pe_seg, wbig.astype(BF16)


def _nsa_cmp(kc_seg, vc_seg, pe_k, w1_k, w2_k, pe_v, w1_v, w2_v, k_gain0, batch, seq):
    nseg = seq // CMP_STRIDE
    segw = CMP_STRIDE * NSA_KV_WIDTH
    kseg = kc_seg.reshape(batch, nseg, segw)
    vseg = vc_seg.reshape(batch, nseg, segw)
    pek, w1k = _cmp_weights(pe_k, w1_k)
    pev, w1v = _cmp_weights(pe_v, w1_v)
    cmp_end = jnp.arange(nseg) * CMP_STRIDE + CMP_LEN - 1
    cos_t, sin_t = _rope_tables(cmp_end, 1)
    hid2 = NSA_KV_HEADS * CMP_HIDDEN
    seg_spec = pl.BlockSpec((1, nseg, segw), lambda b: (b, 0, 0))
    out_spec = pl.BlockSpec((1, NSA_KV_HEADS, nseg, NSA_DIM), lambda b: (b, 0, 0, 0))
    out_shape = jax.ShapeDtypeStruct((batch, NSA_KV_HEADS, nseg, NSA_DIM), BF16)
    out_t_spec = pl.BlockSpec((1, NSA_KV_HEADS, NSA_DIM, nseg), lambda b: (b, 0, 0, 0))
    out_t_shape = jax.ShapeDtypeStruct((batch, NSA_KV_HEADS, NSA_DIM, nseg), BF16)
    return pl.pallas_call(
        _nsa_cmp_kernel,
        grid=(batch,),
        in_specs=[
            seg_spec, seg_spec,
            pl.BlockSpec((2, segw), lambda b: (0, 0)),
            pl.BlockSpec((2, segw), lambda b: (0, 0)),
            pl.BlockSpec((2, segw, hid2), lambda b: (0, 0, 0)),
            pl.BlockSpec((2, segw, hid2), lambda b: (0, 0, 0)),
            pl.BlockSpec((CMP_HIDDEN, NSA_DIM), lambda b: (0, 0)),
            pl.BlockSpec((NSA_DIM, CMP_HIDDEN), lambda b: (0, 0)),
            pl.BlockSpec((1, NSA_DIM), lambda b: (0, 0)),
            pl.BlockSpec((nseg, NSA_DIM), lambda b: (0, 0)),
            pl.BlockSpec((nseg, NSA_DIM), lambda b: (0, 0)),
        ],
        out_specs=[out_spec, out_t_spec],
        out_shape=[out_shape, out_t_shape],
        compiler_params=_params(("parallel",)),
        name="nsa_cmp",
    )(kseg, vseg, pek, pev, w1k, w1v, w2_k.astype(BF16), w2_v.T.astype(BF16),
      k_gain0[None, :], cos_t, sin_t)


def _nsa_attn_kernel(qt_ref, kc_ref, vct_ref, ks_ref, vst_ref, kw_ref, vwt_ref, gate_ref,
                     mcs_ref, bound_ok_ref, o_ref, *, seq, top):
    tq = qt_ref.shape[3]
    tk = min(ATT_K, seq)
    nseg = kc_ref.shape[2]
    nslc = mcs_ref.shape[0]
    t0 = pl.program_id(1) * tq
    groups = range(kc_ref.shape[1])
    qts = [jnp.concatenate([qt_ref[0, g * NSA_GROUP + h] for h in range(NSA_GROUP)], axis=1) for g in groups]

    def tile_heads(x):
        return jnp.concatenate([x] * NSA_GROUP, axis=1)

    blk_n = lax.broadcasted_iota(jnp.int32, (nseg, tq), 0)
    t_col = t0 + lax.broadcasted_iota(jnp.int32, (nseg, tq), 1)
    valid = (blk_n * CMP_STRIDE + (CMP_LEN - 1)) <= t_col
    j_io = lax.broadcasted_iota(jnp.int32, (nslc, tq), 0)
    cur = (t0 + lax.broadcasted_iota(jnp.int32, (nslc, tq), 1)) // SLC_LEN
    forced = (j_io == 0) | (j_io == cur) | (j_io == cur - 1)
    mcs = mcs_ref[...]
    o_cmps = []
    q_aug = []
    for g in groups:
        sc = _dot(kc_ref[0, g], qts[g])
        p_heads = []
        for h in range(NSA_GROUP):
            sm = jnp.where(valid, sc[:, h * tq:(h + 1) * tq], NEG)
            p = jnp.where(valid, jnp.exp2(sm - jnp.max(sm, axis=0, keepdims=True)), 0.0)
            p_heads.append(p / jnp.maximum(jnp.sum(p, axis=0, keepdims=True), 1e-30))
        o_cmps.append(_dot(vct_ref[0, g], jnp.concatenate(p_heads, axis=1).astype(BF16)))

        p_sum = functools.reduce(lambda a, b: a + b, p_heads)
        imp = functools.reduce(lambda a, b: a + b, [_dot(mcs, piece) for piece in _split3(p_sum)])
        score = jnp.where(forced, FORCED_SCORE, jnp.where(j_io <= cur, imp, NEG))
        rank = jnp.zeros((nslc, tq), F32)
        for jp in range(nslc):
            other = score[jp:jp + 1, :]
            ahead = (other > score) | ((other == score) & (j_io > jp))
            rank = rank + jnp.where(ahead, 1.0, 0.0)
        sel_bias = jnp.where(rank < top, 0.0, NEG)
        sel_rows = jnp.concatenate([sel_bias, jnp.zeros((K_AUG - NSA_DIM - nslc, tq), F32)], axis=0)
        q_aug.append(jnp.concatenate([qts[g], tile_heads(sel_rows).astype(BF16)], axis=0))

    lanes = NSA_GROUP * tq
    init1 = (jnp.full((1, lanes), NEG, F32), jnp.zeros((V_AUG, lanes), F32))
    init = tuple((init1, init1) for _ in groups)
    k_io = lax.broadcasted_iota(jnp.int32, (tk, tq), 0)
    t_io = t0 + lax.broadcasted_iota(jnp.int32, (tk, tq), 1)
    diag = t0 // tk
    n_win = -(-(WINDOW - 1) // tk) + 1

    def online_step(carry, s, vt):
        m_old, acc = carry
        m_new = jnp.maximum(m_old, jnp.max(s, axis=0, keepdims=True))
        pe = jnp.exp2(s - m_new).astype(BF16)
        return m_new, jnp.exp2(m_old - m_new) * acc + _dot(vt, pe)

    def bounded_step(carry, s, vt):
        return carry[0], carry[1] + _dot(vt, jnp.exp2(s).astype(BF16))

    def tile_step(kt, carries, window, causal=False, tail=True, span=1, online_step=online_step):
        k0 = pl.multiple_of(kt * tk, tk)
        kpos = k0 + k_io
        out = []
        for g in groups:
            c_slc, c_swa = carries[g]
            if window:
                keys = jnp.concatenate([ks_ref[0, g, pl.ds(k0, tk), :], kw_ref[0, g, pl.ds(k0, tk), :]], axis=0)
                s2 = _dot(keys, q_aug[g])
                s_slc, s_swa = s2[:tk], s2[tk:]
                if causal:
                    s_slc = s_slc + tile_heads(jnp.where(kpos <= t_io, 0.0, NEG))
                if causal and tail:
                    s_swa = s_swa + tile_heads(jnp.where((kpos <= t_io) & (kpos > t_io - WINDOW), 0.0, NEG))
                elif causal:
                    s_swa = s_swa + tile_heads(jnp.where(kpos <= t_io, 0.0, NEG))
                elif tail:
                    s_swa = s_swa + tile_heads(jnp.where(kpos > t_io - WINDOW, 0.0, NEG))
                c_slc = online_step(c_slc, s_slc, vst_ref[0, g, :, pl.ds(k0, tk)])
                c_swa = online_step(c_swa, s_swa, vwt_ref[0, g, :, pl.ds(k0, tk)])
            else:
                s = _dot(ks_ref[0, g, pl.ds(k0, span * tk), :], q_aug[g])
                c_slc = online_step(c_slc, s, vst_ref[0, g, :, pl.ds(k0, span * tk)])
            out.append((c_slc, c_swa))
        return tuple(out)

    rest = jnp.maximum(diag - (n_win - 1), 0)

    def attend(step):
        carries = tile_step(diag, init, True, causal=True, tail=tk - 1 >= WINDOW, online_step=step)
        carries = lax.fori_loop(
            1, jnp.minimum(n_win, diag + 1), lambda i, c: tile_step(diag - i, c, True, online_step=step), carries)
        carries = lax.fori_loop(
            0, rest // 2, lambda p, c: tile_step(rest - 2 - 2 * p, c, False, span=2, online_step=step), carries)
        carries = lax.cond(rest % 2 == 1, lambda c: tile_step(0, c, False, online_step=step), lambda c: c, carries)

        grp_w = NSA_GROUP * NSA_DIM
        for g in groups:
            (_, acc_s), (_, acc_w) = carries[g]
            o_slc = acc_s[:NSA_DIM] / acc_s[NSA_DIM:NSA_DIM + 1]
            o_swa = acc_w[:NSA_DIM] / acc_w[NSA_DIM:NSA_DIM + 1]
            e, kvg = divmod(g, NSA_KV_HEADS)
            gate = _sigmoid(gate_ref[0, e, :, kvg * LANES:(kvg + 1) * LANES].T)
            outs = []
            for h in range(NSA_GROUP):
                ls = slice(h * tq, (h + 1) * tq)
                g0 = gate[N_BRANCH * h:N_BRANCH * h + 1, :]
                g1 = gate[N_BRANCH * h + 1:N_BRANCH * h + 2, :]
                g2 = gate[N_BRANCH * h + 2:N_BRANCH * h + 3, :]
                outs.append(g0 * o_cmps[g][:, ls] + g1 * o_slc[:, ls] + g2 * o_swa[:, ls])
            o_ref[0, e, :, kvg * grp_w:(kvg + 1) * grp_w] = jnp.concatenate(outs, axis=0).T.astype(o_ref.dtype)

    bounded = bound_ok_ref[0] != 0

    @pl.when(bounded)
    def _():
        attend(bounded_step)

    @pl.when(jnp.logical_not(bounded))
    def _():
        attend(online_step)


def _score_bound_ok(q_gain, k_gain):
    k_max = jnp.max(jnp.abs(k_gain[1:]))
    bound = NSA_DIM * jnp.max(jnp.abs(q_gain)) * k_max * (NSA_DIM ** -0.5 * LOG2_E)
    return (bound <= ATT_SAFE_SCORE).astype(jnp.int32).reshape(1)


def _nsa_attn(qt, kc, vct, ks, vst, kw, vwt, gate, bound_ok, batch, seq):
    tq = min(ATT_Q, seq)
    nq = seq // tq
    nseg = seq // CMP_STRIDE
    nslc = seq // SLC_LEN
    top = min(SLC_TOP, nslc)
    c_start = np.arange(nseg) * CMP_STRIDE
    c_end = c_start + CMP_LEN - 1
    s_start = np.arange(nslc) * SLC_LEN
    s_end = s_start + SLC_LEN - 1
    overlap = (c_start[None, :] <= s_end[:, None]) & (c_end[None, :] >= s_start[:, None])
    overlap[:, nseg - 1] = False
    mcs = jnp.asarray(overlap.astype(np.float32), BF16)
    ns = ATT_SEQS if batch % ATT_SEQS == 0 else 1
    steps = batch // ns

    def fold(x):
        return x.reshape((steps, ns * x.shape[1]) + x.shape[2:])

    def seq_block(shape):
        return pl.BlockSpec((1,) + shape, lambda b, i: (b, 0, 0, 0))

    k_full = seq_block((ns * NSA_KV_HEADS, seq, K_AUG))
    vt_full = seq_block((ns * NSA_KV_HEADS, V_AUG, seq))
    out = pl.pallas_call(
        functools.partial(_nsa_attn_kernel, seq=seq, top=top),
        grid=(steps, nq),
        in_specs=[
            pl.BlockSpec((1, ns * NSA_HEADS, NSA_DIM, tq), lambda b, i: (b, 0, 0, i)),
            seq_block((ns * NSA_KV_HEADS, nseg, NSA_DIM)),
            seq_block((ns * NSA_KV_HEADS, NSA_DIM, nseg)),
            k_full, vt_full, k_full, vt_full,
            pl.BlockSpec((1, ns, tq, NSA_KV_HEADS * LANES), lambda b, i: (b, 0, i, 0)),
            pl.BlockSpec((nslc, nseg), lambda b, i: (0, 0)),
            pl.BlockSpec(memory_space=pltpu.SMEM),
        ],
        out_specs=pl.BlockSpec((1, ns, tq, NSA_WIDTH), lambda b, i: (b, 0, i, 0)),
        out_shape=jax.ShapeDtypeStruct((steps, ns, seq, NSA_WIDTH), BF16),
        compiler_params=_params(("parallel", "arbitrary")),
        name="nsa_attn",
    )(fold(qt), fold(kc), fold(vct), fold(ks), fold(vst), fold(kw), fold(vwt),
      gate.reshape(steps, ns, seq, NSA_KV_HEADS * LANES), mcs, bound_ok)
    return out.reshape(batch * seq, NSA_WIDTH)


def _out_ffn_kernel(h_ref, ohg_ref, onsa_ref, wout_ref, gain_ref, wg_ref, wu_ref, wo_ref,
                    o_ref, xn_ref, acc_ref):
    j = pl.program_id(1)

    @pl.when(j == 0)
    def _():
        h1 = (h_ref[...] + _dot(ohg_ref[...], wout_ref[:HG_WIDTH, :])
              + _dot(onsa_ref[...], wout_ref[HG_WIDTH:, :]))
        acc_ref[...] = h1
        ms = jnp.mean(h1 * h1, axis=-1, keepdims=True)
        xn_ref[...] = (h1 * lax.rsqrt(ms + RMS_EPS) * gain_ref[...]).astype(BF16)

    xn = xn_ref[...]
    act = (_silu(_dot(xn, wg_ref[...])) * _dot(xn, wu_ref[...])).astype(BF16)
    acc_ref[...] += _dot(act, wo_ref[...])

    @pl.when(j == pl.num_programs(1) - 1)
    def _():
        o_ref[...] = acc_ref[...]


def _out_ffn(h, o_hg, o_nsa, w_out, gain, w_ffn_in, w_ffn_out):
    n = h.shape[0]
    tm = PROJ_ROWS
    nff = D_FF // FF_TILE
    mixw = HG_WIDTH + NSA_WIDTH
    return pl.pallas_call(
        _out_ffn_kernel,
        grid=(n // tm, nff),
        in_specs=[
            pl.BlockSpec((tm, D_MODEL), lambda i, j: (i, 0)),
            pl.BlockSpec((tm, HG_WIDTH), lambda i, j: (i, 0)),
            pl.BlockSpec((tm, NSA_WIDTH), lambda i, j: (i, 0)),
            pl.BlockSpec((mixw, D_MODEL), lambda i, j: (0, 0)),
            pl.BlockSpec((1, D_MODEL), lambda i, j: (0, 0)),
            pl.BlockSpec((D_MODEL, FF_TILE), lambda i, j: (0, j)),
            pl.BlockSpec((D_MODEL, FF_TILE), lambda i, j: (0, nff + j)),
            pl.BlockSpec((FF_TILE, D_MODEL), lambda i, j: (j, 0)),
        ],
        out_specs=pl.BlockSpec((tm, D_MODEL), lambda i, j: (i, 0)),
        out_shape=jax.ShapeDtypeStruct((n, D_MODEL), F32),
        scratch_shapes=[pltpu.VMEM((tm, D_MODEL), BF16), pltpu.VMEM((tm, D_MODEL), F32)],
        compiler_params=_params(("parallel", "arbitrary")),
        name="out_ffn",
    )(h, o_hg, o_nsa, w_out, gain, w_ffn_in, w_ffn_in, w_ffn_out)


def _pad_w_in(w):
    pieces = [w[:, :_KV_END]]
    for g in range(NSA_KV_HEADS):
        cols = w[:, _KV_END + g * GATES_PER_GROUP:_KV_END + (g + 1) * GATES_PER_GROUP]
        pieces.append(jnp.pad(cols, ((0, 0), (0, LANES - GATES_PER_GROUP))))
    return jnp.concatenate(pieces, axis=1).astype(BF16)


def kernel(x, w_in, w_out, hg_lb_logits, hg_gnorm, q_gain, k_gain, cmp_pe_k, cmp_w1_k, cmp_w2_k,
           cmp_pe_v, cmp_w1_v, cmp_w2_v, w_ffn_in, w_ffn_out, norm_mix, norm_ffn):
    batch, seq, _ = x.shape
    depth = w_in.shape[0]
    h = x.reshape(batch * seq, D_MODEL)
    for l in range(depth):
        hgp, kc_tok, vc_tok, gate, qt, ks, vst, kw, vwt = _norm_proj(
            h, norm_mix[l][None, :], _pad_w_in(w_in[l]), q_gain[l], k_gain[l], batch, seq)
        o_hg = _hgrn(hgp, hg_lb_logits, hg_gnorm[l][None, :], l, batch, seq)
        kc, vct = _nsa_cmp(kc_tok, vc_tok, cmp_pe_k[l], cmp_w1_k[l], cmp_w2_k[l],
                           cmp_pe_v[l], cmp_w1_v[l], cmp_w2_v[l], k_gain[l, 0], batch, seq)
        o_nsa = _nsa_attn(qt, kc, vct, ks, vst, kw, vwt, gate, _score_bound_ok(q_gain[l], k_gain[l]), batch, seq)
        h = _out_ffn(h, o_hg, o_nsa, w_out[l].astype(BF16), norm_ffn[l][None, :],
                     w_ffn_in[l].astype(BF16), w_ffn_out[l].astype(BF16))
    return h.reshape(batch, seq, D_MODEL)
```
